```python
import math
import jax
import jax.numpy as jnp
from jax import lax
import numpy as np

D_MODEL = 1024
BATCH = 2
SEQ = 16384
DEPTH = 4
DEC_BATCH = 8
DEC_SEQ = 64
PAST_LEN = 1024

CHUNK = 64
QBLK = 128
SB_KBLK = 128
HEAD_DIM = 64
D_MIX = D_MODEL
N_MIXERS = 4
GROUP_WIDTH = D_MIX // N_MIXERS
N_HEADS_G = GROUP_WIDTH // HEAD_DIM
MLA_Q_LORA = 384
MLA_KV_LORA = 256
MLA_NOPE = 64
MLA_ROPE = 32
MLA_V = HEAD_DIM
DIFF_HALF = HEAD_DIM // 2
BAND_CHUNKS = 8
BAND_ROWS = BAND_CHUNKS * CHUNK
REL_CLIP = 128
N_GROUPS = 4
EXPERTS_PER_GROUP = 4
TOP_K_IN_GROUP = 2
D_EXPERT = 128
ROPE_THETA = 10000.0
ALPHA = (2 * DEPTH) ** 0.25
BETA_INIT = (8 * DEPTH) ** -0.25
LN_EPS = 1e-5
NEG = -1e30
IN_SPLITS = (MLA_Q_LORA, MLA_KV_LORA, MLA_ROPE) + (GROUP_WIDTH,) * 9
D_IN = MLA_Q_LORA + MLA_KV_LORA + MLA_ROPE + 9 * GROUP_WIDTH

kernel_name = 'hybrid_streaming_encoder_step'


def split_points():
    pts, acc = [], 0
    for w in IN_SPLITS[:-1]:
        acc += w
        pts.append(acc)
    return pts


def layer_norm(x, g, b):
    xf = x.astype(jnp.float32)
    mu = jnp.mean(xf, axis=-1, keepdims=True)
    var = jnp.mean(jnp.square(xf - mu), axis=-1, keepdims=True)
    y = (xf - mu) * lax.rsqrt(var + LN_EPS)
    return (y * g.astype(jnp.float32) + b.astype(jnp.float32)).astype(x.dtype)


def rms_norm(x, w, eps):
    xf = x.astype(jnp.float32)
    y = xf * lax.rsqrt(jnp.mean(jnp.square(xf), axis=-1, keepdims=True) + eps)
    return (y * w.astype(jnp.float32)).astype(x.dtype)


def rope(x, pos):
    d = x.shape[-1]
    inv_freq = ROPE_THETA ** (-jnp.arange(0, d, 2, dtype=jnp.float32) / d)
    ang = pos.astype(jnp.float32)[:, None] * inv_freq[None, :]
    cos = jnp.cos(ang)[None, :, None, :]
    sin = jnp.sin(ang)[None, :, None, :]
    xf = x.astype(jnp.float32)
    x1, x2 = xf[..., : d // 2], xf[..., d // 2:]
    return jnp.concatenate([x1 * cos - x2 * sin, x1 * sin + x2 * cos], axis=-1).astype(x.dtype)


def heads(a):
    return a.reshape(a.shape[0], a.shape[1], N_HEADS_G, HEAD_DIM)


def extend(cache, new, pos):
    if cache is None:
        return new, pos
    return (jnp.concatenate([cache, new], axis=1),
            jnp.arange(cache.shape[1] + new.shape[1], dtype=jnp.int32))


def sweep_causal(fn, q, kv, q_pos, k_pos):
    T = q.shape[1]
    if T <= QBLK or T % QBLK:
        return fn(q, kv, q_pos, k_pos)
    off = k_pos.shape[0] - T
    outs = []
    for i in range(T // QBLK):
        lo, hi = i * QBLK, (i + 1) * QBLK
        outs.append(fn(q[:, lo:hi], tuple(a[:, :off + hi] for a in kv), q_pos[lo:hi], k_pos[:off + hi]))
    return jnp.concatenate(outs, axis=1)


def chunk_causal(q_pos, k_pos):
    return (k_pos[None, :] // CHUNK) <= (q_pos[:, None] // CHUNK)


def softmax_attn(q, k, v, q_pos, k_pos, scale):
    s = jnp.einsum('bqhd,bkhd->bhqk', q, k).astype(jnp.float32) * scale
    s = jnp.where(chunk_causal(q_pos, k_pos)[None, None], s, NEG)
    p = jax.nn.softmax(s, axis=-1)
    return jnp.einsum('bhqk,bkhd->bqhd', p.astype(v.dtype), v)


def stick_breaking_attn(q, k, v, q_pos, k_pos):
    z = jnp.einsum('bqhd,bkhd->bhqk', q, k).astype(jnp.float32) * HEAD_DIM ** -0.5
    B, H, Tq, Tk = z.shape
    earlier = (k_pos[None, :] < q_pos[:, None])[None, None]
    log_keep = jnp.where(earlier, jax.nn.log_sigmoid(-z), 0.0)
    nk = -(-Tk // SB_KBLK)
    lk = jnp.pad(log_keep, ((0, 0), (0, 0), (0, 0), (0, nk * SB_KBLK - Tk))).reshape(B, H, Tq, nk, SB_KBLK)
    after = (jnp.arange(SB_KBLK)[:, None] > jnp.arange(SB_KBLK)[None, :]).astype(jnp.float32)
    within = jnp.einsum('bhqnj,js->bhqns', lk, after)
    tot = jnp.sum(lk, axis=-1)
    later = lax.cumsum(tot, axis=3, reverse=True) - tot
    between = (within + later[..., None]).reshape(B, H, Tq, nk * SB_KBLK)[..., :Tk]
    w = jnp.where(earlier, jnp.exp(jax.nn.log_sigmoid(z) + between), 0.0)
    return jnp.einsum('bhqk,bkhd->bqhd', w.astype(v.dtype), v)


def diff_attn(q, k, v, q_pos, k_pos, lam):
    mask = chunk_causal(q_pos, k_pos)[None, None]

    def smap(qh, kh):
        s = jnp.einsum('bqhd,bkhd->bhqk', qh, kh).astype(jnp.float32) * DIFF_HALF ** -0.5
        return jax.nn.softmax(jnp.where(mask, s, NEG), axis=-1)

    w = smap(q[..., :DIFF_HALF], k[..., :DIFF_HALF]) - lam * smap(q[..., DIFF_HALF:], k[..., DIFF_HALF:])
    return jnp.einsum('bhqk,bkhd->bqhd', w.astype(v.dtype), v)


def band_attn(q, k, v, q_pos, k_pos, rel_table):
    qc = q_pos[:, None] // CHUNK
    kc = k_pos[None, :] // CHUNK
    ok = (k_pos[None, :] >= 0) & (kc <= qc) & (kc >= qc - BAND_CHUNKS)
    rel = jnp.clip(q_pos[:, None] - k_pos[None, :], -REL_CLIP, REL_CLIP) + REL_CLIP
    bias = rel_table.astype(jnp.float32)[:, rel]
    s = jnp.einsum('bqhd,bkhd->bhqk', q, k).astype(jnp.float32) * HEAD_DIM ** -0.5 + bias[None]
    s = jnp.where(ok[None, None], s, NEG)
    p = jax.nn.softmax(s, axis=-1)
    return jnp.einsum('bhqk,bkhd->bqhd', p.astype(v.dtype), v)


def mla_mixer(cq, ckv, kpe, pos, cache_ckv, cache_kpe, q_norm_w, w_uq, kv_norm_w, w_uk, w_uv):
    B, T, _ = cq.shape
    q = (rms_norm(cq, q_norm_w, 1e-6) @ w_uq).reshape(B, T, N_HEADS_G, MLA_NOPE + MLA_ROPE)
    q = jnp.concatenate([q[..., :MLA_NOPE], rope(q[..., MLA_NOPE:], pos)], axis=-1)
    ckv_n = rms_norm(ckv, kv_norm_w, 1e-6)
    kpe_r = rope(kpe[:, :, None, :], pos)[:, :, 0, :]
    ckv_all, k_pos = extend(cache_ckv, ckv_n, pos)
    kpe_all, _ = extend(cache_kpe, kpe_r, pos)
    Tk = ckv_all.shape[1]
    k_nope = (ckv_all @ w_uk).reshape(B, Tk, N_HEADS_G, MLA_NOPE)
    v = (ckv_all @ w_uv).reshape(B, Tk, N_HEADS_G, MLA_V)
    k = jnp.concatenate([k_nope, jnp.broadcast_to(kpe_all[:, :, None, :], (B, Tk, N_HEADS_G, MLA_ROPE))], axis=-1)
    scale = (MLA_NOPE + MLA_ROPE) ** -0.5
    o = sweep_causal(lambda qb, kvb, qp, kp: softmax_attn(qb, kvb[0], kvb[1], qp, kp, scale), q, (k, v), pos, k_pos)
    return o.reshape(B, T, -1), ckv_n, kpe_r


def stick_breaking_mixer(q, k, v, pos, cache_k, cache_v):
    B, T, _ = q.shape
    q, k, v = heads(q), heads(k), heads(v)
    k_all, k_pos = extend(cache_k, k, pos)
    v_all, _ = extend(cache_v, v, pos)
    o = sweep_causal(lambda qb, kvb, qp, kp: stick_breaking_attn(qb, kvb[0], kvb[1], qp, kp), q, (k_all, v_all), pos, k_pos)
    return o.reshape(B, T, -1), k, v


def diff_mixer(q, k, v, pos, cache_k, cache_v, lq1, lk1, lq2, lk2, subln_w, lam_init):
    B, T, _ = q.shape
    q = rope(q.reshape(B, T, 2 * N_HEADS_G, DIFF_HALF), pos).reshape(B, T, N_HEADS_G, HEAD_DIM)
    k = rope(k.reshape(B, T, 2 * N_HEADS_G, DIFF_HALF), pos).reshape(B, T, N_HEADS_G, HEAD_DIM)
    v = heads(v)
    f32 = jnp.float32
    lam = (jnp.exp(jnp.sum(lq1.astype(f32) * lk1.astype(f32)))
           - jnp.exp(jnp.sum(lq2.astype(f32) * lk2.astype(f32))) + lam_init)
    k_all, k_pos = extend(cache_k, k, pos)
    v_all, _ = extend(cache_v, v, pos)
    o = sweep_causal(lambda qb, kvb, qp, kp: diff_attn(qb, kvb[0], kvb[1], qp, kp, lam), q, (k_all, v_all), pos, k_pos)
    o = rms_norm(o, subln_w, 1e-5) * (1.0 - lam_init)
    return o.reshape(B, T, -1), k, v


def band_mixer(q, k, v, pos, cache_k, cache_v, rel_table):
    B, T, _ = q.shape
    q, k, v = heads(q), heads(k), heads(v)
    if cache_k is None:
        n_chunks = T // CHUNK
        pad = jnp.zeros((B, BAND_ROWS, N_HEADS_G, HEAD_DIM), k.dtype)
        k_pad = jnp.concatenate([pad, k], axis=1)
        v_pad = jnp.concatenate([pad, v], axis=1)
        q_chunks = q.reshape(B, n_chunks, CHUNK, N_HEADS_G, HEAD_DIM).swapaxes(0, 1)

        def one_chunk(args):
            qc, c = args
            start = c * CHUNK
            kb = lax.dynamic_slice_in_dim(k_pad, start, BAND_ROWS + CHUNK, axis=1)
            vb = lax.dynamic_slice_in_dim(v_pad, start, BAND_ROWS + CHUNK, axis=1)
            kp = start - BAND_ROWS + jnp.arange(BAND_ROWS + CHUNK, dtype=jnp.int32)
            qp = start + jnp.arange(CHUNK, dtype=jnp.int32)
            return band_attn(qc, kb, vb, qp, kp, rel_table)

        o = lax.map(one_chunk, (q_chunks, jnp.arange(n_chunks, dtype=jnp.int32)))
        o = o.swapaxes(0, 1).reshape(B, T, -1)
        keep = min(BAND_ROWS, T)
        return o, k[:, T - keep:], v[:, T - keep:]
    band_len = cache_k.shape[1]
    kb = jnp.concatenate([cache_k, k], axis=1)
    vb = jnp.concatenate([cache_v, v], axis=1)
    kp = pos[0] - band_len + jnp.arange(band_len + T, dtype=jnp.int32)
    o = band_attn(q, kb, vb, pos, kp, rel_table)
    return o.reshape(B, T, -1), k, v


def routed_ffn(x, w_rg, b_rg, w_re, b_re, w_gate, w_up, w_down):
    B, T, D = x.shape
    xt = x.reshape(-1, D)
    f32 = jnp.float32
    g_logit = (xt @ w_rg).astype(f32) + b_rg.astype(f32)
    g_prob = jax.nn.softmax(g_logit, axis=-1)
    g_onehot = jax.nn.one_hot(jnp.argmax(g_logit, axis=-1), N_GROUPS, dtype=f32)
    g_weight = jnp.sum(g_prob * g_onehot, axis=-1)
    e_logit = ((xt @ w_re).astype(f32) + b_re.astype(f32)).reshape(-1, N_GROUPS, EXPERTS_PER_GROUP)
    e_sel = jnp.einsum('ng,nge->ne', g_onehot, e_logit)
    top_v, top_i = lax.top_k(e_sel, TOP_K_IN_GROUP)
    top_w = jax.nn.softmax(top_v, axis=-1) * g_weight[:, None]
    e_weight = jnp.sum(jax.nn.one_hot(top_i, EXPERTS_PER_GROUP, dtype=f32) * top_w[..., None], axis=1)
    comb = g_onehot[:, :, None] * e_weight[:, None, :]
    out = jnp.zeros_like(xt)
    for g in range(N_GROUPS):
        hg = jax.nn.silu(jnp.einsum('nd,edf->nef', xt, w_gate[g])) * jnp.einsum('nd,edf->nef', xt, w_up[g])
        out = out + jnp.einsum('nef,efd->nd', hg * comb[:, g, :, None].astype(hg.dtype), w_down[g])
    return out.reshape(B, T, D)


def trunk(x, caches, P):
    B, T, _ = x.shape
    past = 0 if caches is None else caches[0].shape[2]
    pos = past + jnp.arange(T, dtype=jnp.int32)
    x = layer_norm(x, P['ln_in_g'], P['ln_in_b'])
    new_rows = [[] for _ in range(8)]
    for l in range(DEPTH):
        c = [None] * 8 if caches is None else [a[l] for a in caches]
        h = x
        parts = jnp.split(h @ P['w_in'][l], split_points(), axis=-1)
        o_a, ckv_n, kpe_r = mla_mixer(parts[0], parts[1], parts[2], pos, c[0], c[1],
                                      P['mla_q_norm'][l], P['mla_w_uq'][l], P['mla_kv_norm'][l],
                                      P['mla_w_uk'][l], P['mla_w_uv'][l])
        o_b, sb_k, sb_v = stick_breaking_mixer(parts[3], parts[4], parts[5], pos, c[2], c[3])
        lam_init = 0.8 - 0.6 * math.exp(-0.3 * l)
        o_c, d_k, d_v = diff_mixer(parts[6], parts[7], parts[8], pos, c[4], c[5],
                                   P['diff_lam_q1'][l], P['diff_lam_k1'][l], P['diff_lam_q2'][l],
                                   P['diff_lam_k2'][l], P['diff_subln'][l], lam_init)
        o_d, b_k, b_v = band_mixer(parts[9], parts[10], parts[11], pos, c[6], c[7], P['band_rel_bias'][l])
        mix = jnp.concatenate([o_a, o_b, o_c, o_d], axis=-1)
        x = layer_norm(ALPHA * h + mix @ P['w_out'][l], P['ln1_g'][l], P['ln1_b'][l])
        f = routed_ffn(x, P['w_router_group'][l], P['b_router_group'][l], P['w_router_expert'][l],
                       P['b_router_expert'][l], P['w_exp_gate'][l], P['w_exp_up'][l], P['w_exp_down'][l])
        x = layer_norm(ALPHA * x + f, P['ln2_g'][l], P['ln2_b'][l])
        for lst, arr in zip(new_rows, (ckv_n, kpe_r, sb_k, sb_v, d_k, d_v, b_k, b_v)):
            lst.append(arr)
    return x, [jnp.stack(lst) for lst in new_rows]


def setup_inputs(seed: int = 0) -> dict:
    key = jax.random.key(seed)
    keys = iter(jax.random.split(key, 48))

    def nrm(shape, scale):
        return jax.random.normal(next(keys), shape, jnp.float32) * scale

    H = N_HEADS_G
    band_len = min(BAND_ROWS, PAST_LEN)
    G, E, F = N_GROUPS, EXPERTS_PER_GROUP, D_EXPERT
    return {
        'x_prompt': nrm((BATCH, SEQ, D_MODEL), 1.0),
        'x_sample': nrm((DEC_BATCH, DEC_SEQ, D_MODEL), 1.0),
        'cache_mla_ckv': nrm((DEPTH, DEC_BATCH, PAST_LEN, MLA_KV_LORA), 1.0),
        'cache_mla_kpe': nrm((DEPTH, DEC_BATCH, PAST_LEN, MLA_ROPE), 1.0),
        'cache_sb_k': nrm((DEPTH, DEC_BATCH, PAST_LEN, H, HEAD_DIM), 1.0),
        'cache_sb_v': nrm((DEPTH, DEC_BATCH, PAST_LEN, H, HEAD_DIM), 1.0),
        'cache_diff_k': nrm((DEPTH, DEC_BATCH, PAST_LEN, H, HEAD_DIM), 1.0),
        'cache_diff_v': nrm((DEPTH, DEC_BATCH, PAST_LEN, H, HEAD_DIM), 1.0),
        'cache_band_k': nrm((DEPTH, DEC_BATCH, band_len, H, HEAD_DIM), 1.0),
        'cache_band_v': nrm((DEPTH, DEC_BATCH, band_len, H, HEAD_DIM), 1.0),
        'ln_in_g': 1.0 + nrm((D_MODEL,), 0.02),
        'ln_in_b': nrm((D_MODEL,), 0.02),
        'w_in': nrm((DEPTH, D_MODEL, D_IN), D_MODEL ** -0.5),
        'mla_q_norm': 1.0 + nrm((DEPTH, MLA_Q_LORA), 0.02),
        'mla_w_uq': nrm((DEPTH, MLA_Q_LORA, H * (MLA_NOPE + MLA_ROPE)), MLA_Q_LORA ** -0.5),
        'mla_kv_norm': 1.0 + nrm((DEPTH, MLA_KV_LORA), 0.02),
        'mla_w_uk': nrm((DEPTH, MLA_KV_LORA, H * MLA_NOPE), MLA_KV_LORA ** -0.5),
        'mla_w_uv': nrm((DEPTH, MLA_KV_LORA, H * MLA_V), MLA_KV_LORA ** -0.5),
        'diff_lam_q1': nrm((DEPTH, DIFF_HALF), 0.1),
        'diff_lam_k1': nrm((DEPTH, DIFF_HALF), 0.1),
        'diff_lam_q2': nrm((DEPTH, DIFF_HALF), 0.1),
        'diff_lam_k2': nrm((DEPTH, DIFF_HALF), 0.1),
        'diff_subln': 1.0 + nrm((DEPTH, HEAD_DIM), 0.02),
        'band_rel_bias': nrm((DEPTH, H, 2 * REL_CLIP + 1), 0.1),
        'w_out': nrm((DEPTH, D_MIX, D_MODEL), D_MIX ** -0.5 * BETA_INIT),
        'ln1_g': 1.0 + nrm((DEPTH, D_MODEL), 0.02),
        'ln1_b': nrm((DEPTH, D_MODEL), 0.02),
        'w_router_group': nrm((DEPTH, D_MODEL, G), D_MODEL ** -0.5),
        'b_router_group': nrm((DEPTH, G), 0.01),
        'w_router_expert': nrm((DEPTH, D_MODEL, G * E), D_MODEL ** -0.5),
        'b_router_expert': nrm((DEPTH, G * E), 0.01),
        'w_exp_gate': nrm((DEPTH, G, E, D_MODEL, F), D_MODEL ** -0.5),
        'w_exp_up': nrm((DEPTH, G, E, D_MODEL, F), D_MODEL ** -0.5),
        'w_exp_down': nrm((DEPTH, G, E, F, D_MODEL), F ** -0.5 * BETA_INIT),
        'ln2_g': 1.0 + nrm((DEPTH, D_MODEL), 0.02),
        'ln2_b': nrm((DEPTH, D_MODEL), 0.02),
    }


def reference(x_prompt, x_sample, cache_mla_ckv, cache_mla_kpe, cache_sb_k, cache_sb_v,
              cache_diff_k, cache_diff_v, cache_band_k, cache_band_v,
              ln_in_g, ln_in_b, w_in, mla_q_norm, mla_w_uq, mla_kv_norm, mla_w_uk, mla_w_uv,
              diff_lam_q1, diff_lam_k1, diff_lam_q2, diff_lam_k2, diff_subln, band_rel_bias,
              w_out, ln1_g, ln1_b, w_router_group, b_router_group, w_router_expert, b_router_expert,
              w_exp_gate, w_exp_up, w_exp_down, ln2_g, ln2_b):
    P = {
        'ln_in_g': ln_in_g, 'ln_in_b': ln_in_b, 'w_in': w_in,
        'mla_q_norm': mla_q_norm, 'mla_w_uq': mla_w_uq, 'mla_kv_norm': mla_kv_norm,
        'mla_w_uk': mla_w_uk, 'mla_w_uv': mla_w_uv,
        'diff_lam_q1': diff_lam_q1, 'diff_lam_k1': diff_lam_k1,
        'diff_lam_q2': diff_lam_q2, 'diff_lam_k2': diff_lam_k2, 'diff_subln': diff_subln,
        'band_rel_bias': band_rel_bias, 'w_out': w_out, 'ln1_g': ln1_g, 'ln1_b': ln1_b,
        'w_router_group': w_router_group, 'b_router_group': b_router_group,
        'w_router_expert': w_router_expert, 'b_router_expert': b_router_expert,
        'w_exp_gate': w_exp_gate, 'w_exp_up': w_exp_up, 'w_exp_down': w_exp_down,
        'ln2_g': ln2_g, 'ln2_b': ln2_b,
    }
    y_prompt, st_p = trunk(x_prompt, None, P)
    caches = (cache_mla_ckv, cache_mla_kpe, cache_sb_k, cache_sb_v,
              cache_diff_k, cache_diff_v, cache_band_k, cache_band_v)
    y_sample, st_s = trunk(x_sample, caches, P)
    p_ckv, p_kpe, p_sb_k, p_sb_v, p_diff_k, p_diff_v, p_band_k, p_band_v = st_p
    s_ckv, s_kpe, s_sb_k, s_sb_v, s_diff_k, s_diff_v, s_band_k, s_band_v = st_s
    return (y_prompt, y_sample,
            p_ckv, p_kpe, p_sb_k, p_sb_v, p_diff_k, p_diff_v, p_band_k, p_band_v,
            s_ckv, s_kpe, s_sb_k, s_sb_v, s_diff_k, s_diff_v, s_band_k, s_band_v)
```

```python
import functools
import math

import jax
import jax.numpy as jnp
from jax import lax
from jax.experimental import pallas as pl
from jax.experimental.pallas import tpu as pltpu

F32 = jnp.float32
BF16 = jnp.bfloat16

CHUNK = 64
HEAD_DIM = 64
N_HEADS = 4
GROUP_WIDTH = N_HEADS * HEAD_DIM
PAIR = 2 * HEAD_DIM
MLA_Q_LORA = 384
MLA_KV_LORA = 256
MLA_NOPE = 64
MLA_ROPE = 32
MLA_QBLK = 256
DIFF_HALF = 32
BAND_ROWS = 512
REL_CLIP = 128
N_GROUPS = 4
EXPERTS_PER_GROUP = 4
N_EXPERTS = N_GROUPS * EXPERTS_PER_GROUP
D_EXPERT = 128
ROPE_THETA = 10000.0
LN_EPS = 1e-5
NEG = -1e30
LANES = 128
SB_DEAD = -104.0
D_IN_PAD = MLA_Q_LORA + MLA_KV_LORA + 9 * GROUP_WIDTH + LANES
VMEM_LIMIT = 56 * 1024 * 1024

_NT = (((1,), (1,)), ((), ()))


def _params(*sem):
    return pltpu.CompilerParams(dimension_semantics=sem, vmem_limit_bytes=VMEM_LIMIT)


def _dot(a, b):
    return jnp.dot(a, b, preferred_element_type=F32)


def _dot_nt(a, b):
    return lax.dot_general(a, b, _NT, preferred_element_type=F32)


def _layer_norm(x, g, b):
    mu = jnp.mean(x, axis=-1, keepdims=True)
    var = jnp.mean(jnp.square(x - mu), axis=-1, keepdims=True)
    return (x - mu) * lax.rsqrt(var + LN_EPS) * g + b


def _rms_norm(x, w, eps):
    return x * lax.rsqrt(jnp.mean(jnp.square(x), axis=-1, keepdims=True) + eps) * w


def _rope(x, cos, sin_signed):
    n = x.shape[-1]
    lane = lax.broadcasted_iota(jnp.int32, x.shape, 1)
    partner = jnp.where((lane & 16) == 0, pltpu.roll(x, n - 16, 1), pltpu.roll(x, 16, 1))
    return x * cos + partner * sin_signed


def _head_lanes(shape, lo, width):
    lane = lax.broadcasted_iota(jnp.int32, shape, 1)
    return (lane >= lo) & (lane < lo + width)


def _row_spec(tm, width):
    return pl.BlockSpec((tm, width), lambda i: (i, 0))


def _const_spec(shape):
    return pl.BlockSpec(shape, lambda *_: (0,) * len(shape))


def _resident_spec(rows, width):
    return pl.BlockSpec((1, rows, width), lambda b, i: (b, 0, 0), pipeline_mode=pl.Buffered(1))


def _ln_kernel(x_ref, g_ref, b_ref, o_ref):
    o_ref[...] = _layer_norm(x_ref[...], g_ref[...], b_ref[...])


def _ln_call(x, g, b, tm):
    n, d = x.shape
    return pl.pallas_call(
        _ln_kernel, grid=(n // tm,),
        in_specs=[_row_spec(tm, d), _const_spec((1, d)), _const_spec((1, d))],
        out_specs=_row_spec(tm, d),
        out_shape=jax.ShapeDtypeStruct((n, d), F32),
        compiler_params=_params("parallel"), name="ln_in",
    )(x, g.reshape(1, d), b.reshape(1, d))


_PRE_F32 = ("ckv", "kpe", "sbk", "sbv", "dk", "dv", "bk", "bv")
_PRE_BF16 = ("sbq", "sbk", "sbv", "dq", "dk", "dv", "bq", "bk", "bv")


def _pre_kernel(x_ref, cos_ref, sin_ref, win_ref, qn_ref, wuq_ref, kvn_ref,
                ckv_o, kpe_o, sbk_o, sbv_o, dk_o, dv_o, bk_o, bv_o,
                qm_o, sbq_h, sbk_h, sbv_h, dq_h, dk_h, dv_h, bq_h, bk_h, bv_h):
    xb = x_ref[...].astype(BF16)
    cos = cos_ref[...]
    sin = sin_ref[...]
    cos_l, sin_l = cos[:, :LANES], sin[:, :LANES]

    def proj(seg, width=GROUP_WIDTH):
        return _dot(xb, win_ref[:, seg:seg + width])

    mla_scale = (MLA_NOPE + MLA_ROPE) ** -0.5
    cq = _rms_norm(proj(0, MLA_Q_LORA), qn_ref[...], 1e-6)
    q = _dot(cq.astype(BF16), wuq_ref[...])
    for h in range(N_HEADS):
        lo = h * MLA_QBLK
        qm_o[:, lo:lo + LANES] = (q[:, lo:lo + LANES] * mla_scale).astype(BF16)
        qr = _rope(q[:, lo + LANES:lo + MLA_QBLK], cos_l, sin_l)
        qm_o[:, lo + LANES:lo + MLA_QBLK] = (qr * mla_scale).astype(BF16)
    seg = MLA_Q_LORA
    ckv_o[...] = _rms_norm(proj(seg), kvn_ref[...], 1e-6)
    seg += MLA_KV_LORA
    kpe_o[...] = _rope(proj(D_IN_PAD - LANES, LANES), cos_l, sin_l)

    sbq_h[...] = (proj(seg) * HEAD_DIM ** -0.5).astype(BF16)
    k = proj(seg + GROUP_WIDTH)
    sbk_o[...] = k
    sbk_h[...] = k.astype(BF16)
    v = proj(seg + 2 * GROUP_WIDTH)
    sbv_o[...] = v
    sbv_h[...] = v.astype(BF16)
    seg += 3 * GROUP_WIDTH
    dq_h[...] = (_rope(proj(seg), cos, sin) * DIFF_HALF ** -0.5).astype(BF16)
    k = _rope(proj(seg + GROUP_WIDTH), cos, sin)
    dk_o[...] = k
    dk_h[...] = k.astype(BF16)
    v = proj(seg + 2 * GROUP_WIDTH)
    dv_o[...] = v
    dv_h[...] = v.astype(BF16)
    seg += 3 * GROUP_WIDTH
    bq_h[...] = (proj(seg) * HEAD_DIM ** -0.5).astype(BF16)
    k = proj(seg + GROUP_WIDTH)
    bk_o[...] = k
    bk_h[...] = k.astype(BF16)
    v = proj(seg + 2 * GROUP_WIDTH)
    bv_o[...] = v
    bv_h[...] = v.astype(BF16)


def _pre_call(x, cos, sin, w, t, tm):
    n, d = x.shape
    nt = t // tm
    tab = pl.BlockSpec((tm, GROUP_WIDTH), lambda i: (i % nt, 0))
    widths_f32 = [GROUP_WIDTH, LANES] + [GROUP_WIDTH] * 6
    widths_h = [N_HEADS * MLA_QBLK] + [GROUP_WIDTH] * 9
    out_shape = ([jax.ShapeDtypeStruct((n, wd), F32) for wd in widths_f32]
                 + [jax.ShapeDtypeStruct((n, wd), BF16) for wd in widths_h])
    outs = pl.pallas_call(
        _pre_kernel, grid=(n // tm,),
        in_specs=[_row_spec(tm, d), tab, tab,
                  _const_spec((d, D_IN_PAD)), _const_spec((1, MLA_Q_LORA)),
                  _const_spec((MLA_Q_LORA, N_HEADS * MLA_QBLK)), _const_spec((1, MLA_KV_LORA))],
        out_specs=[_row_spec(tm, wd) for wd in widths_f32 + widths_h],
        out_shape=out_shape,
        compiler_params=_params("parallel"), name="pre",
    )(x, cos, sin, w["w_in"], w["q_norm"], w["w_uq"], w["kv_norm"])
    f32 = dict(zip(_PRE_F32, outs[:8]))
    h16 = dict(zip(("qm",) + _PRE_BF16, outs[8:]))
    return f32, h16


def _kvup_kernel(ckv_ref, kpe_ref, wuk_ref, wuv_ref, km_o, vm_o):
    cb = ckv_ref[...].astype(BF16)
    kn = _dot(cb, wuk_ref[...]).astype(BF16)
    kpe = kpe_ref[...].astype(BF16)
    for p in range(2):
        km_o[:, p * MLA_QBLK:p * MLA_QBLK + PAIR] = kn[:, p * PAIR:(p + 1) * PAIR]
        km_o[:, p * MLA_QBLK + PAIR:(p + 1) * MLA_QBLK] = kpe
    vm_o[...] = _dot(cb, wuv_ref[...]).astype(BF16)


def _kvup_call(ckv, kpe, w, tm):
    n = ckv.shape[0]
    return pl.pallas_call(
        _kvup_kernel, grid=(n // tm,),
        in_specs=[_row_spec(tm, MLA_KV_LORA), _row_spec(tm, LANES),
                  _const_spec((MLA_KV_LORA, GROUP_WIDTH)), _const_spec((MLA_KV_LORA, GROUP_WIDTH))],
        out_specs=[_row_spec(tm, 2 * MLA_QBLK), _row_spec(tm, GROUP_WIDTH)],
        out_shape=[jax.ShapeDtypeStruct((n, 2 * MLA_QBLK), BF16),
                   jax.ShapeDtypeStruct((n, GROUP_WIDTH), BF16)],
        compiler_params=_params("parallel"), name="kvup",
    )(ckv, kpe, w["w_uk"], w["w_uv"])


def _chunk_mask(i, jd, tq, tk, off):
    qpos = off + i * tq + lax.broadcasted_iota(jnp.int32, (tq, tk), 0)
    kpos = jd * tk + lax.broadcasted_iota(jnp.int32, (tq, tk), 1)
    return (kpos // CHUNK) <= (qpos // CHUNK)


def _online_softmax(state, s, v):
    m, l, acc = state
    m_new = jnp.maximum(m, jnp.max(s, axis=-1, keepdims=True))
    a = jnp.exp(m - m_new)
    p = jnp.exp(s - m_new)
    return (m_new, a * l + jnp.sum(p, axis=-1, keepdims=True), a * acc + _dot(p.astype(BF16), v))


def _softmax_init(tq):
    return (jnp.full((tq, 1), NEG, F32), jnp.zeros((tq, 1), F32), jnp.zeros((tq, PAIR), F32))


def _mla_kernel(q_ref, k_ref, v_ref, o_ref, *, tq, tk, off):
    i = pl.program_id(1)
    jd = (off + i * tq) // tk
    qs = [q_ref[0, :, h * MLA_QBLK:(h + 1) * MLA_QBLK] for h in range(N_HEADS)]

    def step(j, states, mask):
        start = pl.multiple_of(j * tk, tk)
        new = []
        for p in range(2):
            kp = k_ref[0, pl.ds(start, tk), p * MLA_QBLK:(p + 1) * MLA_QBLK]
            vp = v_ref[0, pl.ds(start, tk), p * PAIR:(p + 1) * PAIR]
            for hh in range(2):
                s = _dot_nt(qs[2 * p + hh], kp)
                if mask is not None:
                    s = jnp.where(mask, s, NEG)
                new.append(_online_softmax(states[2 * p + hh], s, vp))
        return tuple(new)

    states = lax.fori_loop(0, jd, lambda j, st: step(j, st, None),
                           tuple(_softmax_init(tq) for _ in range(N_HEADS)))
    states = step(jd, states, _chunk_mask(i, jd, tq, tk, off))
    first = _head_lanes((tq, PAIR), 0, HEAD_DIM)
    for p in range(2):
        (_, l0, a0), (_, l1, a1) = states[2 * p], states[2 * p + 1]
        o_ref[0, :, p * PAIR:(p + 1) * PAIR] = jnp.where(first, a0 / l0, a1 / l1).astype(BF16)


def _diff_kernel(q_ref, k_ref, v_ref, lam_ref, subln_ref, o_ref, *, tq, tk, off, lam_init):
    i = pl.program_id(1)
    jd = (off + i * tq) // tk
    qs = []
    for p in range(2):
        qp = q_ref[0, :, p * PAIR:(p + 1) * PAIR]
        for hh in range(2):
            for half in range(2):
                sel = _head_lanes((tq, PAIR), hh * HEAD_DIM + half * DIFF_HALF, DIFF_HALF)
                qs.append(jnp.where(sel, qp, jnp.zeros_like(qp)))

    def step(j, states, mask):
        start = pl.multiple_of(j * tk, tk)
        new = []
        for p in range(2):
            kp = k_ref[0, pl.ds(start, tk), p * PAIR:(p + 1) * PAIR]
            vp = v_ref[0, pl.ds(start, tk), p * PAIR:(p + 1) * PAIR]
            for n in range(4 * p, 4 * p + 4):
                s = _dot_nt(qs[n], kp)
                if mask is not None:
                    s = jnp.where(mask, s, NEG)
                new.append(_online_softmax(states[n], s, vp))
        return tuple(new)

    states = lax.fori_loop(0, jd, lambda j, st: step(j, st, None),
                           tuple(_softmax_init(tq) for _ in range(2 * N_HEADS)))
    states = step(jd, states, _chunk_mask(i, jd, tq, tk, off))

    lp = lam_ref[...]
    lam = (jnp.exp(jnp.sum(lp[0:1] * lp[1:2], axis=-1, keepdims=True))
           - jnp.exp(jnp.sum(lp[2:3] * lp[3:4], axis=-1, keepdims=True)) + lam_init)
    gain = subln_ref[...]
    for p in range(2):
        out = jnp.zeros((tq, PAIR), F32)
        for hh in range(2):
            (_, l1, a1), (_, l2, a2) = states[4 * p + 2 * hh], states[4 * p + 2 * hh + 1]
            own = _head_lanes((tq, PAIR), hh * HEAD_DIM, HEAD_DIM)
            o = jnp.where(own, a1 / l1 - lam * (a2 / l2), 0.0)
            ms = jnp.sum(o * o, axis=-1, keepdims=True) / HEAD_DIM
            out = out + o * lax.rsqrt(ms + 1e-5) * gain * (1.0 - lam_init)
        o_ref[0, :, p * PAIR:(p + 1) * PAIR] = out.astype(BF16)


def _sb_kernel(q_ref, k_ref, v_ref, o_ref, *, tq, tk, off):
    i = pl.program_id(1)
    jd = (off + i * tq) // tk
    row = lax.broadcasted_iota(jnp.int32, (tk, tk), 0)
    col = lax.broadcasted_iota(jnp.int32, (tk, tk), 1)
    after = jnp.where(row > col, 1.0, 0.0).astype(BF16)
    qpos = off + i * tq + lax.broadcasted_iota(jnp.int32, (tq, tk), 0)
    kcol = lax.broadcasted_iota(jnp.int32, (tq, tk), 1)
    first = _head_lanes((tq, PAIR), 0, HEAD_DIM)

    for p in range(2):
        qp = q_ref[0, :, p * PAIR:(p + 1) * PAIR]
        accs = []
        for hh in range(2):
            qh = jnp.where(_head_lanes((tq, PAIR), hh * HEAD_DIM, HEAD_DIM), qp, jnp.zeros_like(qp))

            def block(j, later, acc, diagonal, qh=qh, p=p):
                start = pl.multiple_of(j * tk, tk)
                kp = k_ref[0, pl.ds(start, tk), p * PAIR:(p + 1) * PAIR]
                vp = v_ref[0, pl.ds(start, tk), p * PAIR:(p + 1) * PAIR]
                z = _dot_nt(qh, kp)
                t = jnp.log1p(jnp.exp(-jnp.abs(z)))
                log_keep = -(jnp.maximum(z, 0.0) + t)
                log_beta = jnp.minimum(z, 0.0) - t
                if diagonal:
                    earlier = (start + kcol) < qpos
                    log_keep = jnp.where(earlier, log_keep, 0.0)
                hi = log_keep.astype(BF16)
                lo = (log_keep - hi.astype(F32)).astype(BF16)
                within = _dot(hi, after) + _dot(lo, after)
                w = jnp.exp(log_beta + within + later)
                if diagonal:
                    w = jnp.where(earlier, w, 0.0)
                acc = acc + _dot(w.astype(BF16), vp)
                later = later + jnp.sum(log_keep, axis=-1, keepdims=True)
                return later, acc

            def alive(later):
                return (jnp.max(later) > SB_DEAD).astype(jnp.int32)

            later, acc = block(jd, jnp.zeros((tq, 1), F32), jnp.zeros((tq, PAIR), F32), True)

            def body(c, block=block):
                j, _, later, acc = c
                later, acc = block(j, later, acc, False)
                return j - 1, alive(later), later, acc

            _, _, _, acc = lax.while_loop(lambda c: (c[0] >= 0) & (c[1] > 0), body,
                                          (jd - 1, alive(later), later, acc))
            accs.append(acc)
        o_ref[0, :, p * PAIR:(p + 1) * PAIR] = jnp.where(first, accs[0], accs[1]).astype(BF16)


def _causal_call(kernel, q, k, v, extra, extra_specs, tq, tk, off, name):
    b, t, qw = q.shape
    tk_all = k.shape[1]
    assert off % tk == 0 and tk % tq == 0 and t % tq == 0 and tk_all % tk == 0
    assert off + t <= tk_all
    return pl.pallas_call(
        functools.partial(kernel, tq=tq, tk=tk, off=off), grid=(b, t // tq),
        in_specs=[pl.BlockSpec((1, tq, qw), lambda b_, i: (b_, i, 0)),
                  _resident_spec(tk_all, k.shape[2]), _resident_spec(tk_all, v.shape[2])] + extra_specs,
        out_specs=pl.BlockSpec((1, tq, GROUP_WIDTH), lambda b_, i: (b_, i, 0)),
        out_shape=jax.ShapeDtypeStruct((b, t, GROUP_WIDTH), BF16),
        compiler_params=_params("parallel", "arbitrary"), name=name,
    )(q, k, v, *extra)


def _band_kernel(tab_ref, q_ref, k_ref, v_ref, o_ref, bias_ref, *, tq, win, pos0):
    b = pl.program_id(0)
    i = pl.program_id(1)

    @pl.when((b == 0) & (i == 0))
    def _():
        rows = 8

        def fill(c, carry):
            r0 = pl.multiple_of(c * rows, rows)
            r = r0 + lax.broadcasted_iota(jnp.int32, (rows, win), 0)
            u = lax.broadcasted_iota(jnp.int32, (rows, win), 1)
            rel = jnp.clip(r - u + BAND_ROWS, -REL_CLIP, REL_CLIP) + REL_CLIP
            dchunk = r // CHUNK - u // CHUNK + BAND_ROWS // CHUNK
            ok = (dchunk >= 0) & (dchunk <= BAND_ROWS // CHUNK)

            def pick(d, vals):
                hit = rel == d
                return tuple(jnp.where(hit, tab_ref[h, d], vals[h]) for h in range(N_HEADS))

            vals = lax.fori_loop(0, 2 * REL_CLIP + 1, pick,
                                 tuple(jnp.zeros((rows, win), F32) for _ in range(N_HEADS)))
            for h in range(N_HEADS):
                bias_ref[h, pl.ds(r0, rows), :] = jnp.where(ok, vals[h], NEG)
            return carry

        lax.fori_loop(0, tq // rows, fill, 0)

    start = pl.multiple_of(i * tq, tq)
    u = lax.broadcasted_iota(jnp.int32, (tq, win), 1)
    exists = (pos0 + i * tq - BAND_ROWS + u) >= 0
    first = _head_lanes((tq, PAIR), 0, HEAD_DIM)
    for p in range(2):
        qp = q_ref[0, :, p * PAIR:(p + 1) * PAIR]
        kw = k_ref[0, pl.ds(start, win), p * PAIR:(p + 1) * PAIR]
        vw = v_ref[0, pl.ds(start, win), p * PAIR:(p + 1) * PAIR]
        outs = []
        for hh in range(2):
            qh = jnp.where(_head_lanes((tq, PAIR), hh * HEAD_DIM, HEAD_DIM), qp, jnp.zeros_like(qp))
            s = jnp.where(exists, _dot_nt(qh, kw) + bias_ref[2 * p + hh], NEG)
            e = jnp.exp(s - jnp.max(s, axis=-1, keepdims=True))
            outs.append(_dot(e.astype(BF16), vw) / jnp.sum(e, axis=-1, keepdims=True))
        o_ref[0, :, p * PAIR:(p + 1) * PAIR] = jnp.where(first, outs[0], outs[1]).astype(BF16)


def _band_call(q, k_pad, v_pad, rel_table, tq, win, pos0):
    b, t, _ = q.shape
    rows = k_pad.shape[1]
    assert t % tq == 0 and tq % CHUNK == 0 and win % LANES == 0 and win >= tq + BAND_ROWS
    assert rows >= t - tq + win and pos0 % CHUNK == 0
    return pl.pallas_call(
        functools.partial(_band_kernel, tq=tq, win=win, pos0=pos0), grid=(b, t // tq),
        in_specs=[pl.BlockSpec(memory_space=pltpu.SMEM),
                  pl.BlockSpec((1, tq, GROUP_WIDTH), lambda b_, i: (b_, i, 0)),
                  _resident_spec(rows, GROUP_WIDTH), _resident_spec(rows, GROUP_WIDTH)],
        out_specs=pl.BlockSpec((1, tq, GROUP_WIDTH), lambda b_, i: (b_, i, 0)),
        out_shape=jax.ShapeDtypeStruct((b, t, GROUP_WIDTH), BF16),
        scratch_shapes=[pltpu.VMEM((N_HEADS, tq, win), F32)],
        compiler_params=_params("arbitrary", "arbitrary"), name="band",
    )(rel_table, q, k_pad, v_pad)


def _post_kernel(h_ref, oa_ref, ob_ref, oc_ref, od_ref, wout_ref, g1_ref, b1_ref,
                 wr_hi_ref, wr_lo_ref, br_ref, wg_ref, wu_ref, wd_ref, g2_ref, b2_ref, o_ref, *, alpha):
    tm = h_ref.shape[0]
    y = jnp.zeros(h_ref.shape, F32)
    for g, ref in enumerate((oa_ref, ob_ref, oc_ref, od_ref)):
        y = y + _dot(ref[...], wout_ref[g * GROUP_WIDTH:(g + 1) * GROUP_WIDTH, :])
    x = _layer_norm(alpha * h_ref[...] + y, g1_ref[...], b1_ref[...])

    x_hi = x.astype(BF16)
    x_lo = (x - x_hi.astype(F32)).astype(BF16)
    logit = (_dot(x_hi, wr_hi_ref[...]) + _dot(x_lo, wr_hi_ref[...]) + _dot(x_hi, wr_lo_ref[...])
             + br_ref[...])
    lane = lax.broadcasted_iota(jnp.int32, (tm, LANES), 1).astype(F32)
    ninf = -jnp.inf

    def lane_max(a):
        return jnp.max(a, axis=-1, keepdims=True)

    def first_lane(hit):
        return jnp.min(jnp.where(hit, lane, float(LANES)), axis=-1, keepdims=True)

    gl = jnp.where(lane < N_GROUPS, logit, ninf)
    g_max = lane_max(gl)
    g_idx = first_lane(gl == g_max)
    g_weight = 1.0 / jnp.sum(jnp.exp(gl - g_max), axis=-1, keepdims=True)
    e_lo = N_GROUPS + EXPERTS_PER_GROUP * g_idx
    el = jnp.where((lane >= e_lo) & (lane < e_lo + EXPERTS_PER_GROUP), logit, ninf)
    v1 = lane_max(el)
    i1 = first_lane(el == v1)
    el2 = jnp.where(lane == i1, ninf, el)
    v2 = lane_max(el2)
    i2 = first_lane(el2 == v2)
    e2 = jnp.exp(v2 - v1)
    w1 = g_weight / (1.0 + e2)
    w2 = g_weight * e2 / (1.0 + e2)
    comb = jnp.where(lane == i1, w1, 0.0) + jnp.where(lane == i2, w2, 0.0)

    xb = x_hi
    gate = _dot(xb, wg_ref[...])
    up = _dot(xb, wu_ref[...])
    hidden = []
    for e in range(N_EXPERTS):
        c_e = jnp.sum(jnp.where(lane == float(N_GROUPS + e), comb, 0.0), axis=-1, keepdims=True)
        ge = gate[:, e * D_EXPERT:(e + 1) * D_EXPERT]
        ue = up[:, e * D_EXPERT:(e + 1) * D_EXPERT]
        hidden.append(((ge * (1.0 / (1.0 + jnp.exp(-ge)))) * ue * c_e).astype(BF16))
    f = _dot(jnp.concatenate(hidden, axis=-1), wd_ref[...])
    o_ref[...] = _layer_norm(alpha * x + f, g2_ref[...], b2_ref[...])


def _post_call(h, mix, w, tm, alpha):
    n, d = h.shape
    de = N_EXPERTS * D_EXPERT
    return pl.pallas_call(
        functools.partial(_post_kernel, alpha=alpha), grid=(n // tm,),
        in_specs=[_row_spec(tm, d)] + [_row_spec(tm, GROUP_WIDTH)] * 4
        + [_const_spec((d, d)), _const_spec((1, d)), _const_spec((1, d)),
           _const_spec((d, LANES)), _const_spec((d, LANES)), _const_spec((1, LANES)),
           _const_spec((d, de)), _const_spec((d, de)), _const_spec((de, d)),
           _const_spec((1, d)), _const_spec((1, d))],
        out_specs=_row_spec(tm, d),
        out_shape=jax.ShapeDtypeStruct((n, d), F32),
        compiler_params=_params("parallel"), name="post",
    )(h, *mix, w["w_out"], w["ln1_g"], w["ln1_b"], w["wr_hi"], w["wr_lo"], w["b_r"],
      w["w_gate"], w["w_up"], w["w_down"], w["ln2_g"], w["ln2_b"])


def _layer_weights(l, P):
    d = P["w_in"].shape[1]
    w_in = P["w_in"][l]
    kpe_lo = MLA_Q_LORA + MLA_KV_LORA
    w_in_pad = jnp.concatenate(
        [w_in[:, :kpe_lo], w_in[:, kpe_lo + MLA_ROPE:], w_in[:, kpe_lo:kpe_lo + MLA_ROPE],
         jnp.zeros((d, LANES - MLA_ROPE), F32)], axis=1)
    w_uq = P["mla_w_uq"][l]
    blocks = []
    for h in range(N_HEADS):
        src = h * (MLA_NOPE + MLA_ROPE)
        nope = w_uq[:, src:src + MLA_NOPE]
        zero = jnp.zeros_like(nope)
        blocks += ([nope, zero] if h % 2 == 0 else [zero, nope])
        blocks += [w_uq[:, src + MLA_NOPE:src + MLA_NOPE + MLA_ROPE],
                   jnp.zeros((MLA_Q_LORA, LANES - MLA_ROPE), F32)]
    w_r = jnp.concatenate([P["w_router_group"][l], P["w_router_expert"][l],
                           jnp.zeros((d, LANES - N_GROUPS - N_EXPERTS), F32)], axis=1)
    wr_hi = w_r.astype(BF16)
    b_r = jnp.concatenate([P["b_router_group"][l], P["b_router_expert"][l],
                           jnp.zeros((LANES - N_GROUPS - N_EXPERTS,), F32)]).reshape(1, LANES)
    de = N_EXPERTS * D_EXPERT
    return {
        "w_in": w_in_pad.astype(BF16),
        "q_norm": P["mla_q_norm"][l].reshape(1, -1),
        "w_uq": jnp.concatenate(blocks, axis=1).astype(BF16),
        "kv_norm": P["mla_kv_norm"][l].reshape(1, -1),
        "w_uk": P["mla_w_uk"][l].astype(BF16),
        "w_uv": P["mla_w_uv"][l].astype(BF16),
        "lam": jnp.stack([P["diff_lam_q1"][l], P["diff_lam_k1"][l],
                          P["diff_lam_q2"][l], P["diff_lam_k2"][l]]),
        "subln": jnp.tile(P["diff_subln"][l], 2).reshape(1, PAIR),
        "rel": P["band_rel_bias"][l],
        "w_out": P["w_out"][l].astype(BF16),
        "ln1_g": P["ln1_g"][l].reshape(1, -1), "ln1_b": P["ln1_b"][l].reshape(1, -1),
        "wr_hi": wr_hi, "wr_lo": (w_r - wr_hi.astype(F32)).astype(BF16), "b_r": b_r,
        "w_gate": P["w_exp_gate"][l].transpose(2, 0, 1, 3).reshape(d, de).astype(BF16),
        "w_up": P["w_exp_up"][l].transpose(2, 0, 1, 3).reshape(d, de).astype(BF16),
        "w_down": P["w_exp_down"][l].reshape(de, d).astype(BF16),
        "ln2_g": P["ln2_g"][l].reshape(1, -1), "ln2_b": P["ln2_b"][l].reshape(1, -1),
    }


def _rope_tables(pos):
    inv_freq = ROPE_THETA ** (-jnp.arange(0, MLA_ROPE, 2, dtype=F32) / MLA_ROPE)
    ang = pos.astype(F32)[:, None] * inv_freq[None, :]
    cos, sin = jnp.cos(ang), jnp.sin(ang)
    reps = GROUP_WIDTH // MLA_ROPE
    return (jnp.tile(jnp.concatenate([cos, cos], axis=1), (1, reps)),
            jnp.tile(jnp.concatenate([-sin, sin], axis=1), (1, reps)))


def _flat_heads(a):
    return a.reshape(a.shape[0], a.shape[1], GROUP_WIDTH)


def _pad_rows(a, rows):
    return jnp.pad(a, ((0, 0), (0, rows - a.shape[1]), (0, 0)))


def _trunk(x, caches, P, weights):
    b, t, d = x.shape
    depth = len(weights)
    alpha = (2 * depth) ** 0.25
    n = b * t
    past = 0 if caches is None else caches[0].shape[2]
    if caches is None:
        tm, tq, tk, band_tq = 256, 256, 256, 256
        tk_all = t
    else:
        tm, tq, tk, band_tq = t, t, 2 * t, t
        tk_all = -(-(past + t) // tk) * tk
        assert caches[6].shape[2] == BAND_ROWS
    assert t >= BAND_ROWS or caches is not None
    band_win = -(-(band_tq + BAND_ROWS) // LANES) * LANES
    band_rows = t - band_tq + band_win
    cos, sin = _rope_tables(past + jnp.arange(t, dtype=jnp.int32))
    h = _ln_call(x.reshape(n, d), P["ln_in_g"], P["ln_in_b"], tm)
    new_rows = [[] for _ in range(8)]
    for l in range(depth):
        w = weights[l]
        f32, h16 = _pre_call(h, cos, sin, w, t, tm)

        def new3(a):
            return a.reshape(b, t, a.shape[-1])

        if caches is None:
            def keys(name, cache_idx):
                return new3(h16[name])
            ckv_all, kpe_all = f32["ckv"], f32["kpe"]
            band_k = jnp.pad(new3(h16["bk"]), ((0, 0), (BAND_ROWS, band_rows - BAND_ROWS - t), (0, 0)))
            band_v = jnp.pad(new3(h16["bv"]), ((0, 0), (BAND_ROWS, band_rows - BAND_ROWS - t), (0, 0)))
        else:
            c = [a[l] for a in caches]

            def keys(name, cache_idx):
                old = _flat_heads(c[cache_idx]).astype(BF16)
                return _pad_rows(jnp.concatenate([old, new3(h16[name])], axis=1), tk_all)
            ckv_all = _pad_rows(jnp.concatenate([c[0], new3(f32["ckv"])], axis=1), tk_all)
            ckv_all = ckv_all.reshape(b * tk_all, MLA_KV_LORA)
            kpe_old = jnp.pad(c[1], ((0, 0), (0, 0), (0, LANES - MLA_ROPE)))
            kpe_all = _pad_rows(jnp.concatenate([kpe_old, new3(f32["kpe"])], axis=1), tk_all)
            kpe_all = kpe_all.reshape(b * tk_all, LANES)
            band_k = _pad_rows(jnp.concatenate([_flat_heads(c[6]).astype(BF16), new3(h16["bk"])], axis=1), band_rows)
            band_v = _pad_rows(jnp.concatenate([_flat_heads(c[7]).astype(BF16), new3(h16["bv"])], axis=1), band_rows)

        km, vm = _kvup_call(ckv_all, kpe_all, w, tm if caches is None else tk)
        o_a = _causal_call(_mla_kernel, new3(h16["qm"]), km.reshape(b, tk_all, -1), vm.reshape(b, tk_all, -1),
                           [], [], tq, tk, past, "mla")
        o_b = _causal_call(_sb_kernel, new3(h16["sbq"]), keys("sbk", 2), keys("sbv", 3), [], [], tq, tk, past, "sb")
        lam_init = 0.8 - 0.6 * math.exp(-0.3 * l)
        o_c = _causal_call(functools.partial(_diff_kernel, lam_init=lam_init), new3(h16["dq"]),
                           keys("dk", 4), keys("dv", 5), [w["lam"], w["subln"]],
                           [_const_spec((4, DIFF_HALF)), _const_spec((1, PAIR))], tq, tk, past, "diff")
        o_d = _band_call(new3(h16["bq"]), band_k, band_v, w["rel"], band_tq, band_win, past)
        mix = [o.reshape(n, GROUP_WIDTH) for o in (o_a, o_b, o_c, o_d)]
        h = _post_call(h, mix, w, tm, alpha)

        keep = min(BAND_ROWS, t)
        rows = (new3(f32["ckv"]), new3(f32["kpe"])[:, :, :MLA_ROPE],
                new3(f32["sbk"]), new3(f32["sbv"]), new3(f32["dk"]), new3(f32["dv"]),
                new3(f32["bk"])[:, t - keep:], new3(f32["bv"])[:, t - keep:])
        for i, (lst, arr) in enumerate(zip(new_rows, rows)):
            lst.append(arr if i < 2 else arr.reshape(b, arr.shape[1], N_HEADS, HEAD_DIM))
    return h.reshape(b, t, d), [jnp.stack(lst) for lst in new_rows]


def kernel(x_prompt, x_sample, cache_mla_ckv, cache_mla_kpe, cache_sb_k, cache_sb_v, cache_diff_k, cache_diff_v, cache_band_k, cache_band_v, ln_in_g, ln_in_b, w_in, mla_q_norm, mla_w_uq, mla_kv_norm, mla_w_uk, mla_w_uv, diff_lam_q1, diff_lam_k1, diff_lam_q2, diff_lam_k2, diff_subln, band_rel_bias, w_out, ln1_g, ln1_b, w_router_group, b_router_group, w_router_expert, b_router_expert, w_exp_gate, w_exp_up, w_exp_down, ln2_g, ln2_b):
    P = {
        "ln_in_g": ln_in_g, "ln_in_b": ln_in_b, "w_in": w_in,
        "mla_q_norm": mla_q_norm, "mla_w_uq": mla_w_uq, "mla_kv_norm": mla_kv_norm,
        "mla_w_uk": mla_w_uk, "mla_w_uv": mla_w_uv,
        "diff_lam_q1": diff_lam_q1, "diff_lam_k1": diff_lam_k1,
        "diff_lam_q2": diff_lam_q2, "diff_lam_k2": diff_lam_k2, "diff_subln": diff_subln,
        "band_rel_bias": band_rel_bias, "w_out": w_out, "ln1_g": ln1_g, "ln1_b": ln1_b,
        "w_router_group": w_router_group, "b_router_group": b_router_group,
        "w_router_expert": w_router_expert, "b_router_expert": b_router_expert,
        "w_exp_gate": w_exp_gate, "w_exp_up": w_exp_up, "w_exp_down": w_exp_down,
        "ln2_g": ln2_g, "ln2_b": ln2_b,
    }
    weights = [_layer_weights(l, P) for l in range(w_in.shape[0])]
    y_prompt, st_p = _trunk(x_prompt, None, P, weights)
    caches = (cache_mla_ckv, cache_mla_kpe, cache_sb_k, cache_sb_v,
              cache_diff_k, cache_diff_v, cache_band_k, cache_band_v)
    y_sample, st_s = _trunk(x_sample, caches, P, weights)
    return (y_prompt, y_sample, *st_p, *st_s)
```

```python
import functools
import math

import jax
import jax.numpy as jnp
from jax import lax
from jax.experimental import pallas as pl
from jax.experimental.pallas import tpu as pltpu

F32 = jnp.float32
BF16 = jnp.bfloat16

CHUNK = 64
HEAD_DIM = 64
N_HEADS = 4
GROUP_WIDTH = N_HEADS * HEAD_DIM
PAIR = 2 * HEAD_DIM
MLA_Q_LORA = 384
MLA_KV_LORA = 256
MLA_NOPE = 64
MLA_ROPE = 32
MLA_QBLK = 256
DIFF_HALF = 32
BAND_ROWS = 512
REL_CLIP = 128
N_GROUPS = 4
EXPERTS_PER_GROUP = 4
N_EXPERTS = N_GROUPS * EXPERTS_PER_GROUP
D_EXPERT = 128
ROPE_THETA = 10000.0
LN_EPS = 1e-5
NEG = -1e30
LANES = 128
SB_DEAD = -104.0
V_WIDE = N_HEADS * LANES
KPE_COL = MLA_Q_LORA + MLA_KV_LORA + 9 * GROUP_WIDTH
DV_WIDE_COL = KPE_COL + LANES
D_IN_PAD = DV_WIDE_COL + V_WIDE
MLA_ONE_LANE = PAIR + MLA_ROPE
EXP_GUARD = 60.0
LOG2E = 1.4426950408889634
VMEM_LIMIT = 56 * 1024 * 1024

_NT = (((1,), (1,)), ((), ()))


def _params(*sem):
    return pltpu.CompilerParams(dimension_semantics=sem, vmem_limit_bytes=VMEM_LIMIT)


def _dot(a, b):
    return jnp.dot(a, b, preferred_element_type=F32)


def _dot_nt(a, b):
    return lax.dot_general(a, b, _NT, preferred_element_type=F32)


def _layer_norm(x, g, b):
    mu = jnp.mean(x, axis=-1, keepdims=True)
    var = jnp.mean(jnp.square(x - mu), axis=-1, keepdims=True)
    return (x - mu) * lax.rsqrt(var + LN_EPS) * g + b


def _rms_norm(x, w, eps):
    return x * lax.rsqrt(jnp.mean(jnp.square(x), axis=-1, keepdims=True) + eps) * w


def _rope(x, cos, sin_signed):
    n = x.shape[-1]
    lane = lax.broadcasted_iota(jnp.int32, x.shape, 1)
    partner = jnp.where((lane & 16) == 0, pltpu.roll(x, n - 16, 1), pltpu.roll(x, 16, 1))
    return x * cos + partner * sin_signed


def _ones_column(shape):
    lane = lax.broadcasted_iota(jnp.int32, shape, 1)
    return jnp.where((lane & (LANES - 1)) == HEAD_DIM, 1.0, 0.0)


def _head_lanes(shape, lo, width):
    lane = lax.broadcasted_iota(jnp.int32, shape, 1)
    return (lane >= lo) & (lane < lo + width)


def _row_spec(tm, width):
    return pl.BlockSpec((tm, width), lambda i: (i, 0))


def _const_spec(shape):
    return pl.BlockSpec(shape, lambda *_: (0,) * len(shape))


def _resident_spec(rows, width):
    return pl.BlockSpec((1, rows, width), lambda b, i: (b, 0, 0), pipeline_mode=pl.Buffered(1))


def _ln_kernel(x_ref, g_ref, b_ref, o_ref):
    o_ref[...] = _layer_norm(x_ref[...], g_ref[...], b_ref[...])


def _ln_call(x, g, b, tm):
    n, d = x.shape
    return pl.pallas_call(
        _ln_kernel, grid=(n // tm,),
        in_specs=[_row_spec(tm, d), _const_spec((1, d)), _const_spec((1, d))],
        out_specs=_row_spec(tm, d),
        out_shape=jax.ShapeDtypeStruct((n, d), F32),
        compiler_params=_params("parallel"), name="ln_in",
    )(x, g.reshape(1, d), b.reshape(1, d))


_PRE_F32 = ("ckv", "kpe", "sbk", "sbv", "dk", "dv", "bk", "bv")
_PRE_BF16 = ("sbq", "sbk", "sbv", "dq", "dk", "dv", "bq", "bk", "bv")


def _pre_kernel(x_ref, cos_ref, sin_ref, win_ref, qn_ref, wuq_ref, kvn_ref,
                ckv_o, kpe_o, sbk_o, sbv_o, dk_o, dv_o, bk_o, bv_o,
                qm_o, sbq_h, sbk_h, sbv_h, dq_h, dk_h, dv_h, bq_h, bk_h, bv_h):
    xb = x_ref[...].astype(BF16)
    cos = cos_ref[...]
    sin = sin_ref[...]
    cos_l, sin_l = cos[:, :LANES], sin[:, :LANES]

    def proj(seg, width=GROUP_WIDTH):
        return _dot(xb, win_ref[:, seg:seg + width])

    mla_scale = (MLA_NOPE + MLA_ROPE) ** -0.5 * LOG2E
    cq =_rms_norm(proj(0, MLA_Q_LORA), qn_ref[...], 1e-6)
    q = _dot(cq.astype(BF16), wuq_ref[...])
    for h in range(N_HEADS):
        lo = h * MLA_QBLK
        qm_o[:, lo:lo + LANES] = (q[:, lo:lo + LANES] * mla_scale).astype(BF16)
        qr = _rope(q[:, lo + LANES:lo + MLA_QBLK], cos_l, sin_l)
        qm_o[:, lo + LANES:lo + MLA_QBLK] = (qr * mla_scale).astype(BF16)
    seg = MLA_Q_LORA
    ckv_o[...] = _rms_norm(proj(seg), kvn_ref[...], 1e-6)
    seg += MLA_KV_LORA
    kpe_o[...] = _rope(proj(KPE_COL, LANES), cos_l, sin_l)

    sbq_h[...] = (proj(seg) * HEAD_DIM ** -0.5).astype(BF16)
    k = proj(seg + GROUP_WIDTH)
    sbk_o[...] = k
    sbk_h[...] = k.astype(BF16)
    v = proj(seg + 2 * GROUP_WIDTH)
    sbv_o[...] = v
    sbv_h[...] = v.astype(BF16)
    seg += 3 * GROUP_WIDTH
    dq_h[...] = (_rope(proj(seg), cos, sin) * (DIFF_HALF ** -0.5 * LOG2E)).astype(BF16)
    k = _rope(proj(seg + GROUP_WIDTH), cos, sin)
    dk_o[...] = k
    dk_h[...] = k.astype(BF16)
    dv_o[...] = proj(seg + 2 * GROUP_WIDTH)
    dv_h[...] = (proj(DV_WIDE_COL, V_WIDE) + _ones_column((xb.shape[0], V_WIDE))).astype(BF16)
    seg += 3 * GROUP_WIDTH
    bq_h[...] = (proj(seg) * HEAD_DIM ** -0.5).astype(BF16)
    k = proj(seg + GROUP_WIDTH)
    bk_o[...] = k
    bk_h[...] = k.astype(BF16)
    v = proj(seg + 2 * GROUP_WIDTH)
    bv_o[...] = v
    bv_h[...] = v.astype(BF16)


def _pre_call(x, cos, sin, w, t, tm):
    n, d = x.shape
    nt = t // tm
    tab = pl.BlockSpec((tm, GROUP_WIDTH), lambda i: (i % nt, 0))
    widths_f32 = [GROUP_WIDTH, LANES] + [GROUP_WIDTH] * 6
    widths_h = [N_HEADS * MLA_QBLK] + [GROUP_WIDTH] * 9
    widths_h[1 + _PRE_BF16.index("dv")] = V_WIDE
    out_shape = ([jax.ShapeDtypeStruct((n, wd), F32) for wd in widths_f32]
                 + [jax.ShapeDtypeStruct((n, wd), BF16) for wd in widths_h])
    outs = pl.pallas_call(
        _pre_kernel, grid=(n // tm,),
        in_specs=[_row_spec(tm, d), tab, tab,
                  _const_spec((d, D_IN_PAD)), _const_spec((1, MLA_Q_LORA)),
                  _const_spec((MLA_Q_LORA, N_HEADS * MLA_QBLK)), _const_spec((1, MLA_KV_LORA))],
        out_specs=[_row_spec(tm, wd) for wd in widths_f32 + widths_h],
        out_shape=out_shape,
        compiler_params=_params("parallel"), name="pre",
    )(x, cos, sin, w["w_in"], w["q_norm"], w["w_uq"], w["kv_norm"])
    f32 = dict(zip(_PRE_F32, outs[:8]))
    h16 = dict(zip(("qm",) + _PRE_BF16, outs[8:]))
    return f32, h16


def _kvup_kernel(ckv_ref, kpe_ref, wuk_ref, wuv_ref, km_o, vm_o):
    cb = ckv_ref[...].astype(BF16)
    kn = _dot(cb, wuk_ref[...]).astype(BF16)
    lane = lax.broadcasted_iota(jnp.int32, kpe_ref.shape, 1)
    kpe = jnp.where(lane == MLA_ROPE, 1.0, kpe_ref[...]).astype(BF16)
    for p in range(2):
        km_o[:, p * MLA_QBLK:p * MLA_QBLK + PAIR] = kn[:, p * PAIR:(p + 1) * PAIR]
        km_o[:, p * MLA_QBLK + PAIR:(p + 1) * MLA_QBLK] = kpe
    vm_o[...] = (_dot(cb, wuv_ref[...]) + _ones_column(vm_o.shape)).astype(BF16)


def _kvup_call(ckv, kpe, w, tm):
    n = ckv.shape[0]
    return pl.pallas_call(
        _kvup_kernel, grid=(n // tm,),
        in_specs=[_row_spec(tm, MLA_KV_LORA), _row_spec(tm, LANES),
                  _const_spec((MLA_KV_LORA, GROUP_WIDTH)), _const_spec((MLA_KV_LORA, V_WIDE))],
        out_specs=[_row_spec(tm, 2 * MLA_QBLK), _row_spec(tm, V_WIDE)],
        out_shape=[jax.ShapeDtypeStruct((n, 2 * MLA_QBLK), BF16),
                   jax.ShapeDtypeStruct((n, V_WIDE), BF16)],
        compiler_params=_params("parallel"), name="kvup",
    )(ckv, kpe, w["w_uk"], w["w_uv"])


def _chunk_mask(i, jd, tq, tk, off):
    qpos = off + i * tq + lax.broadcasted_iota(jnp.int32, (tq, tk), 0)
    kpos = jd * tk + lax.broadcasted_iota(jnp.int32, (tq, tk), 1)
    return (kpos // CHUNK) <= (qpos // CHUNK)


def _fold_max(d):
    m = d[:, :LANES]
    for c in range(1, d.shape[1] // LANES):
        m = jnp.maximum(m, d[:, c * LANES:(c + 1) * LANES])
    return jnp.max(m.reshape(m.shape[0] // 8, 8, LANES), axis=0)


def _softmax_chains(chains, operands, mask, jd, tk, qs_ref, acc_ref):
    start_d = pl.multiple_of(jd * tk, tk)
    for n in range(chains):
        q, k, _ = operands(n, start_d)
        m0 = jnp.max(jnp.where(mask, _dot_nt(q, k), NEG), axis=-1, keepdims=True)
        lane = lax.broadcasted_iota(jnp.int32, q.shape, 1)
        qs_ref[n] = jnp.where(lane == operands(n, None), (-m0).astype(BF16), q)
    acc_ref[...] = jnp.zeros(acc_ref.shape, F32)

    def block(j, top, diagonal):
        start = pl.multiple_of(j * tk, tk)
        for n in range(chains):
            _, k, v = operands(n, start)
            d = _dot_nt(qs_ref[n], k)
            if diagonal:
                d = jnp.where(mask, d, NEG)
            top = jnp.maximum(top, _fold_max(d))
            acc_ref[n] += _dot(jnp.exp2(d).astype(BF16), v)
        return top

    top = lax.fori_loop(0, jd, lambda j, top: block(j, top, False), jnp.full((8, LANES), NEG, F32))
    top = block(jd, top, True)

    @pl.when(jnp.max(top) > EXP_GUARD)
    def _():
        def safe(j, states, diagonal):
            start = pl.multiple_of(j * tk, tk)
            new = []
            for n in range(chains):
                q, k, v = operands(n, start)
                s = _dot_nt(q, k)
                if diagonal:
                    s = jnp.where(mask, s, NEG)
                m, acc = states[n]
                m_new = jnp.maximum(m, jnp.max(s, axis=-1, keepdims=True))
                new.append((m_new, jnp.exp2(m - m_new) * acc + _dot(jnp.exp2(s - m_new).astype(BF16), v)))
            return tuple(new)

        tq = mask.shape[0]
        init = tuple((jnp.full((tq, 1), NEG, F32), jnp.zeros((tq, LANES), F32)) for _ in range(chains))
        states = safe(jd, lax.fori_loop(0, jd, lambda j, st: safe(j, st, False), init), True)
        for n in range(chains):
            acc_ref[n] = states[n][1]


def _flash_scratch(chains, tq):
    return [pltpu.VMEM((chains, tq, MLA_QBLK), BF16), pltpu.VMEM((chains, tq, LANES), F32)]


def _mla_kernel(q_ref, k_ref, v_ref, o_ref, qs_ref, acc_ref, *, tq, tk, off):
    i = pl.program_id(1)
    jd = (off + i * tq) // tk

    def operands(h, start):
        if start is None:
            return MLA_ONE_LANE
        p = h // 2
        return (q_ref[0, :, h * MLA_QBLK:(h + 1) * MLA_QBLK],
                k_ref[0, pl.ds(start, tk), p * MLA_QBLK:(p + 1) * MLA_QBLK],
                v_ref[0, pl.ds(start, tk), h * LANES:(h + 1) * LANES])

    _softmax_chains(N_HEADS, operands, _chunk_mask(i, jd, tq, tk, off), jd, tk, qs_ref, acc_ref)
    outs = []
    for h in range(N_HEADS):
        acc = acc_ref[h]
        outs.append(acc[:, :HEAD_DIM] / acc[:, HEAD_DIM:HEAD_DIM + 1])
    o_ref[0] = jnp.concatenate(outs, axis=-1).astype(BF16)


def _diff_kernel(q_ref, k_ref, v_ref, lam_ref, subln_ref, o_ref, qs_ref, acc_ref, *, tq, tk, off, lam_init):
    i = pl.program_id(1)
    jd = (off + i * tq) // tk

    def operands(n, start):
        if start is None:
            return PAIR
        h, half = n // 2, n % 2
        p, hh = h // 2, h % 2
        qp = q_ref[0, :, p * PAIR:(p + 1) * PAIR]
        sel = _head_lanes((tq, PAIR), hh * HEAD_DIM + half * DIFF_HALF, DIFF_HALF)
        zero = jnp.zeros_like(qp)
        one = jnp.where(lax.broadcasted_iota(jnp.int32, (tk, PAIR), 1) == 0, 1.0, 0.0).astype(BF16)
        return (jnp.concatenate([jnp.where(sel, qp, zero), zero], axis=-1),
                jnp.concatenate([k_ref[0, pl.ds(start, tk), p * PAIR:(p + 1) * PAIR], one], axis=-1),
                v_ref[0, pl.ds(start, tk), h * LANES:(h + 1) * LANES])

    _softmax_chains(2 * N_HEADS, operands, _chunk_mask(i, jd, tq, tk, off), jd, tk, qs_ref, acc_ref)

    lp = lam_ref[...]
    lam = (jnp.exp(jnp.sum(lp[0:1] * lp[1:2], axis=-1, keepdims=True))
           - jnp.exp(jnp.sum(lp[2:3] * lp[3:4], axis=-1, keepdims=True)) + lam_init)
    gain = subln_ref[...]
    outs = []
    for h in range(N_HEADS):
        a1, a2 = acc_ref[2 * h], acc_ref[2 * h + 1]
        o = (a1[:, :HEAD_DIM] / a1[:, HEAD_DIM:HEAD_DIM + 1]
             - lam * (a2[:, :HEAD_DIM] / a2[:, HEAD_DIM:HEAD_DIM + 1]))
        ms = jnp.mean(o * o, axis=-1, keepdims=True)
        outs.append(o * lax.rsqrt(ms + 1e-5) * gain * (1.0 - lam_init))
    o_ref[0] = jnp.concatenate(outs, axis=-1).astype(BF16)


def _sb_kernel(q_ref, k_ref, v_ref, o_ref, *, tq, tk, off):
    i = pl.program_id(1)
    jd = (off + i * tq) // tk
    row = lax.broadcasted_iota(jnp.int32, (tk, tk), 0)
    col = lax.broadcasted_iota(jnp.int32, (tk, tk), 1)
    after = jnp.where(row > col, 1.0, 0.0).astype(BF16)
    qpos = off + i * tq + lax.broadcasted_iota(jnp.int32, (tq, tk), 0)
    kcol = lax.broadcasted_iota(jnp.int32, (tq, tk), 1)
    first = _head_lanes((tq, PAIR), 0, HEAD_DIM)

    for p in range(2):
        qp = q_ref[0, :, p * PAIR:(p + 1) * PAIR]
        accs = []
        for hh in range(2):
            qh = jnp.where(_head_lanes((tq, PAIR), hh * HEAD_DIM, HEAD_DIM), qp, jnp.zeros_like(qp))

            def block(j, later, acc, diagonal, qh=qh, p=p):
                start = pl.multiple_of(j * tk, tk)
                kp = k_ref[0, pl.ds(start, tk), p * PAIR:(p + 1) * PAIR]
                vp = v_ref[0, pl.ds(start, tk), p * PAIR:(p + 1) * PAIR]
                z = _dot_nt(qh, kp)
                t = jnp.log1p(jnp.exp(-jnp.abs(z)))
                log_keep = -(jnp.maximum(z, 0.0) + t)
                log_beta = jnp.minimum(z, 0.0) - t
                if diagonal:
                    earlier = (start + kcol) < qpos
                    log_keep = jnp.where(earlier, log_keep, 0.0)
                hi = log_keep.astype(BF16)
                lo = (log_keep - hi.astype(F32)).astype(BF16)
                within = _dot(hi, after) + _dot(lo, after)
                w = jnp.exp(log_beta + within + later)
                if diagonal:
                    w = jnp.where(earlier, w, 0.0)
                acc = acc + _dot(w.astype(BF16), vp)
                later = later + jnp.sum(log_keep, axis=-1, keepdims=True)
                return later, acc

            def alive(later):
                return (jnp.max(later) > SB_DEAD).astype(jnp.int32)

            later, acc = block(jd, jnp.zeros((tq, 1), F32), jnp.zeros((tq, PAIR), F32), True)

            def body(c, block=block):
                j, _, later, acc = c
                later, acc = block(j, later, acc, False)
                return j - 1, alive(later), later, acc

            _, _, _, acc = lax.while_loop(lambda c: (c[0] >= 0) & (c[1] > 0), body,
                                          (jd - 1, alive(later), later, acc))
            accs.append(acc)
        o_ref[0, :, p * PAIR:(p + 1) * PAIR] = jnp.where(first, accs[0], accs[1]).astype(BF16)


def _causal_call(kernel, q, k, v, extra, extra_specs, tq, tk, off, name, chains=0):
    b, t, qw = q.shape
    tk_all = k.shape[1]
    assert off % tk == 0 and tk % tq == 0 and t % tq == 0 and tk_all % tk == 0
    assert off + t <= tk_all
    return pl.pallas_call(
        functools.partial(kernel, tq=tq, tk=tk, off=off), grid=(b, t // tq),
        in_specs=[pl.BlockSpec((1, tq, qw), lambda b_, i: (b_, i, 0)),
                  _resident_spec(tk_all, k.shape[2]), _resident_spec(tk_all, v.shape[2])] + extra_specs,
        out_specs=pl.BlockSpec((1, tq, GROUP_WIDTH), lambda b_, i: (b_, i, 0)),
        out_shape=jax.ShapeDtypeStruct((b, t, GROUP_WIDTH), BF16),
        scratch_shapes=_flash_scratch(chains, tq) if chains else [],
        compiler_params=_params("parallel", "arbitrary"), name=name,
    )(q, k, v, *extra)


def _band_kernel(tab_ref, q_ref, k_ref, v_ref, o_ref, bias_ref, *, tq, win, pos0):
    b = pl.program_id(0)
    i = pl.program_id(1)

    @pl.when((b == 0) & (i == 0))
    def _():
        rows = 8

        def fill(c, carry):
            r0 = pl.multiple_of(c * rows, rows)
            r = r0 + lax.broadcasted_iota(jnp.int32, (rows, win), 0)
            u = lax.broadcasted_iota(jnp.int32, (rows, win), 1)
            rel = jnp.clip(r - u + BAND_ROWS, -REL_CLIP, REL_CLIP) + REL_CLIP
            dchunk = r // CHUNK - u // CHUNK + BAND_ROWS // CHUNK
            ok = (dchunk >= 0) & (dchunk <= BAND_ROWS // CHUNK)

            def pick(d, vals):
                hit = rel == d
                return tuple(jnp.where(hit, tab_ref[h, d], vals[h]) for h in range(N_HEADS))

            vals = lax.fori_loop(0, 2 * REL_CLIP + 1, pick,
                                 tuple(jnp.zeros((rows, win), F32) for _ in range(N_HEADS)))
            for h in range(N_HEADS):
                bias_ref[h, pl.ds(r0, rows), :] = jnp.where(ok, vals[h], NEG)
            return carry

        lax.fori_loop(0, tq // rows, fill, 0)

    start = pl.multiple_of(i * tq, tq)
    u = lax.broadcasted_iota(jnp.int32, (tq, win), 1)
    exists = (pos0 + i * tq - BAND_ROWS + u) >= 0
    first = _head_lanes((tq, PAIR), 0, HEAD_DIM)
    for p in range(2):
        qp = q_ref[0, :, p * PAIR:(p + 1) * PAIR]
        kw = k_ref[0, pl.ds(start, win), p * PAIR:(p + 1) * PAIR]
        vw = v_ref[0, pl.ds(start, win), p * PAIR:(p + 1) * PAIR]
        outs = []
        for hh in range(2):
            qh = jnp.where(_head_lanes((tq, PAIR), hh * HEAD_DIM, HEAD_DIM), qp, jnp.zeros_like(qp))
            s = jnp.where(exists, _dot_nt(qh, kw) + bias_ref[2 * p + hh], NEG)
            e = jnp.exp(s - jnp.max(s, axis=-1, keepdims=True))
            outs.append(_dot(e.astype(BF16), vw) / jnp.sum(e, axis=-1, keepdims=True))
        o_ref[0, :, p * PAIR:(p + 1) * PAIR] = jnp.where(first, outs[0], outs[1]).astype(BF16)


def _band_call(q, k_pad, v_pad, rel_table, tq, win, pos0):
    b, t, _ = q.shape
    rows = k_pad.shape[1]
    assert t % tq == 0 and tq % CHUNK == 0 and win % LANES == 0 and win >= tq + BAND_ROWS
    assert rows >= t - tq + win and pos0 % CHUNK == 0
    return pl.pallas_call(
        functools.partial(_band_kernel, tq=tq, win=win, pos0=pos0), grid=(b, t // tq),
        in_specs=[pl.BlockSpec(memory_space=pltpu.SMEM),
                  pl.BlockSpec((1, tq, GROUP_WIDTH), lambda b_, i: (b_, i, 0)),
                  _resident_spec(rows, GROUP_WIDTH), _resident_spec(rows, GROUP_WIDTH)],
        out_specs=pl.BlockSpec((1, tq, GROUP_WIDTH), lambda b_, i: (b_, i, 0)),
        out_shape=jax.ShapeDtypeStruct((b, t, GROUP_WIDTH), BF16),
        scratch_shapes=[pltpu.VMEM((N_HEADS, tq, win), F32)],
        compiler_params=_params("arbitrary", "arbitrary"), name="band",
    )(rel_table, q, k_pad, v_pad)


def _post_kernel(h_ref, oa_ref, ob_ref, oc_ref, od_ref, wout_ref, g1_ref, b1_ref,
                 wr_hi_ref, wr_lo_ref, br_ref, wg_ref, wu_ref, wd_ref, g2_ref, b2_ref, o_ref, *, alpha):
    tm = h_ref.shape[0]
    y = jnp.zeros(h_ref.shape, F32)
    for g, ref in enumerate((oa_ref, ob_ref, oc_ref, od_ref)):
        y = y + _dot(ref[...], wout_ref[g * GROUP_WIDTH:(g + 1) * GROUP_WIDTH, :])
    x = _layer_norm(alpha * h_ref[...] + y, g1_ref[...], b1_ref[...])

    x_hi = x.astype(BF16)
    x_lo = (x - x_hi.astype(F32)).astype(BF16)
    logit = (_dot(x_hi, wr_hi_ref[...]) + _dot(x_lo, wr_hi_ref[...]) + _dot(x_hi, wr_lo_ref[...])
             + br_ref[...])
    lane = lax.broadcasted_iota(jnp.int32, (tm, LANES), 1).astype(F32)
    ninf = -jnp.inf

    def lane_max(a):
        return jnp.max(a, axis=-1, keepdims=True)

    def first_lane(hit):
        return jnp.min(jnp.where(hit, lane, float(LANES)), axis=-1, keepdims=True)

    gl = jnp.where(lane < N_GROUPS, logit, ninf)
    g_max = lane_max(gl)
    g_idx = first_lane(gl == g_max)
    g_weight = 1.0 / jnp.sum(jnp.exp(gl - g_max), axis=-1, keepdims=True)
    e_lo = N_GROUPS + EXPERTS_PER_GROUP * g_idx
    el = jnp.where((lane >= e_lo) & (lane < e_lo + EXPERTS_PER_GROUP), logit, ninf)
    v1 = lane_max(el)
    i1 = first_lane(el == v1)
    el2 = jnp.where(lane == i1, ninf, el)
    v2 = lane_max(el2)
    i2 = first_lane(el2 == v2)
    e2 = jnp.exp(v2 - v1)
    w1 = g_weight / (1.0 + e2)
    w2 = g_weight * e2 / (1.0 + e2)
    comb = jnp.where(lane == i1, w1, 0.0) + jnp.where(lane == i2, w2, 0.0)

    xb = x_hi
    gate = _dot(xb, wg_ref[...])
    up = _dot(xb, wu_ref[...])
    hidden = []
    for e in range(N_EXPERTS):
        c_e = jnp.sum(jnp.where(lane == float(N_GROUPS + e), comb, 0.0), axis=-1, keepdims=True)
        ge = gate[:, e * D_EXPERT:(e + 1) * D_EXPERT]
        ue = up[:, e * D_EXPERT:(e + 1) * D_EXPERT]
        hidden.append(((ge * (1.0 / (1.0 + jnp.exp(-ge)))) * ue * c_e).astype(BF16))
    f = _dot(jnp.concatenate(hidden, axis=-1), wd_ref[...])
    o_ref[...] = _layer_norm(alpha * x + f, g2_ref[...], b2_ref[...])


def _post_call(h, mix, w, tm, alpha):
    n, d = h.shape
    de = N_EXPERTS * D_EXPERT
    return pl.pallas_call(
        functools.partial(_post_kernel, alpha=alpha), grid=(n // tm,),
        in_specs=[_row_spec(tm, d)] + [_row_spec(tm, GROUP_WIDTH)] * 4
        + [_const_spec((d, d)), _const_spec((1, d)), _const_spec((1, d)),
           _const_spec((d, LANES)), _const_spec((d, LANES)), _const_spec((1, LANES)),
           _const_spec((d, de)), _const_spec((d, de)), _const_spec((de, d)),
           _const_spec((1, d)), _const_spec((1, d))],
        out_specs=_row_spec(tm, d),
        out_shape=jax.ShapeDtypeStruct((n, d), F32),
        compiler_params=_params("parallel"), name="post",
    )(h, *mix, w["w_out"], w["ln1_g"], w["ln1_b"], w["wr_hi"], w["wr_lo"], w["b_r"],
      w["w_gate"], w["w_up"], w["w_down"], w["ln2_g"], w["ln2_b"])


def _wide_columns(w):
    rows = w.shape[0]
    w = w.reshape(rows, N_HEADS, HEAD_DIM)
    return jnp.concatenate([w, jnp.zeros_like(w)], axis=-1).reshape(rows, V_WIDE)


def _wide_values(v):
    one = jnp.ones(v.shape[:-1] + (1,), v.dtype)
    zero = jnp.zeros(v.shape[:-1] + (LANES - HEAD_DIM - 1,), v.dtype)
    return jnp.concatenate([v, one, zero], axis=-1).reshape(v.shape[0], v.shape[1], V_WIDE)


def _layer_weights(l, P):
    d = P["w_in"].shape[1]
    w_in = P["w_in"][l]
    kpe_lo = MLA_Q_LORA + MLA_KV_LORA
    dv_lo = kpe_lo + MLA_ROPE + 5 * GROUP_WIDTH
    w_in_pad = jnp.concatenate(
        [w_in[:, :kpe_lo], w_in[:, kpe_lo + MLA_ROPE:], w_in[:, kpe_lo:kpe_lo + MLA_ROPE],
         jnp.zeros((d, LANES - MLA_ROPE), F32), _wide_columns(w_in[:, dv_lo:dv_lo + GROUP_WIDTH])], axis=1)
    w_uq = P["mla_w_uq"][l]
    blocks = []
    for h in range(N_HEADS):
        src = h * (MLA_NOPE + MLA_ROPE)
        nope = w_uq[:, src:src + MLA_NOPE]
        zero = jnp.zeros_like(nope)
        blocks += ([nope, zero] if h % 2 == 0 else [zero, nope])
        blocks += [w_uq[:, src + MLA_NOPE:src + MLA_NOPE + MLA_ROPE],
                   jnp.zeros((MLA_Q_LORA, LANES - MLA_ROPE), F32)]
    w_r = jnp.concatenate([P["w_router_group"][l], P["w_router_expert"][l],
                           jnp.zeros((d, LANES - N_GROUPS - N_EXPERTS), F32)], axis=1)
    wr_hi = w_r.astype(BF16)
    b_r = jnp.concatenate([P["b_router_group"][l], P["b_router_expert"][l],
                           jnp.zeros((LANES - N_GROUPS - N_EXPERTS,), F32)]).reshape(1, LANES)
    de = N_EXPERTS * D_EXPERT
    return {
        "w_in": w_in_pad.astype(BF16),
        "q_norm": P["mla_q_norm"][l].reshape(1, -1),
        "w_uq": jnp.concatenate(blocks, axis=1).astype(BF16),
        "kv_norm": P["mla_kv_norm"][l].reshape(1, -1),
        "w_uk": P["mla_w_uk"][l].astype(BF16),
        "w_uv": _wide_columns(P["mla_w_uv"][l]).astype(BF16),
        "lam": jnp.stack([P["diff_lam_q1"][l], P["diff_lam_k1"][l],
                          P["diff_lam_q2"][l], P["diff_lam_k2"][l]]),
        "subln": P["diff_subln"][l].reshape(1, HEAD_DIM),
        "rel": P["band_rel_bias"][l],
        "w_out": P["w_out"][l].astype(BF16),
        "ln1_g": P["ln1_g"][l].reshape(1, -1), "ln1_b": P["ln1_b"][l].reshape(1, -1),
        "wr_hi": wr_hi, "wr_lo": (w_r - wr_hi.astype(F32)).astype(BF16), "b_r": b_r,
        "w_gate": P["w_exp_gate"][l].transpose(2, 0, 1, 3).reshape(d, de).astype(BF16),
        "w_up": P["w_exp_up"][l].transpose(2, 0, 1, 3).reshape(d, de).astype(BF16),
        "w_down": P["w_exp_down"][l].reshape(de, d).astype(BF16),
        "ln2_g": P["ln2_g"][l].reshape(1, -1), "ln2_b": P["ln2_b"][l].reshape(1, -1),
    }


def _rope_tables(pos):
    inv_freq = ROPE_THETA ** (-jnp.arange(0, MLA_ROPE, 2, dtype=F32) / MLA_ROPE)
    ang = pos.astype(F32)[:, None] * inv_freq[None, :]
    cos, sin = jnp.cos(ang), jnp.sin(ang)
    reps = GROUP_WIDTH // MLA_ROPE
    return (jnp.tile(jnp.concatenate([cos, cos], axis=1), (1, reps)),
            jnp.tile(jnp.concatenate([-sin, sin], axis=1), (1, reps)))


def _flat_heads(a):
    return a.reshape(a.shape[0], a.shape[1], GROUP_WIDTH)


def _pad_rows(a, rows):
    return jnp.pad(a, ((0, 0), (0, rows - a.shape[1]), (0, 0)))


def _trunk(x, caches, P, weights):
    b, t, d = x.shape
    depth = len(weights)
    alpha = (2 * depth) ** 0.25
    n = b * t
    past = 0 if caches is None else caches[0].shape[2]
    if caches is None:
        tm, tq, tk, band_tq = 256, 512, 512, 256
        sb_tq = sb_tk = 256
        tk_all = t
    else:
        tm, tq, tk, band_tq = t, t, 2 * t, t
        sb_tq, sb_tk = tq, tk
        tk_all = -(-(past + t) // tk) * tk
        assert caches[6].shape[2] == BAND_ROWS
    assert t >= BAND_ROWS or caches is not None
    band_win = -(-(band_tq + BAND_ROWS) // LANES) * LANES
    band_rows = t - band_tq + band_win
    cos, sin = _rope_tables(past + jnp.arange(t, dtype=jnp.int32))
    h = _ln_call(x.reshape(n, d), P["ln_in_g"], P["ln_in_b"], tm)
    new_rows = [[] for _ in range(8)]
    for l in range(depth):
        w = weights[l]
        f32, h16 = _pre_call(h, cos, sin, w, t, tm)

        def new3(a):
            return a.reshape(b, t, a.shape[-1])

        if caches is None:
            def keys(name, cache_idx):
                return new3(h16[name])
            diff_v = new3(h16["dv"])
            ckv_all, kpe_all = f32["ckv"], f32["kpe"]
            band_k = jnp.pad(new3(h16["bk"]), ((0, 0), (BAND_ROWS, band_rows - BAND_ROWS - t), (0, 0)))
            band_v = jnp.pad(new3(h16["bv"]), ((0, 0), (BAND_ROWS, band_rows - BAND_ROWS - t), (0, 0)))
        else:
            c = [a[l] for a in caches]

            def keys(name, cache_idx):
                old = _flat_heads(c[cache_idx]).astype(BF16)
                return _pad_rows(jnp.concatenate([old, new3(h16[name])], axis=1), tk_all)
            diff_v = _pad_rows(jnp.concatenate([_wide_values(c[5]).astype(BF16), new3(h16["dv"])], axis=1), tk_all)
            ckv_all =_pad_rows(jnp.concatenate([c[0], new3(f32["ckv"])], axis=1), tk_all)
            ckv_all = ckv_all.reshape(b * tk_all, MLA_KV_LORA)
            kpe_old = jnp.pad(c[1], ((0, 0), (0, 0), (0, LANES - MLA_ROPE)))
            kpe_all = _pad_rows(jnp.concatenate([kpe_old, new3(f32["kpe"])], axis=1), tk_all)
            kpe_all = kpe_all.reshape(b * tk_all, LANES)
            band_k = _pad_rows(jnp.concatenate([_flat_heads(c[6]).astype(BF16), new3(h16["bk"])], axis=1), band_rows)
            band_v = _pad_rows(jnp.concatenate([_flat_heads(c[7]).astype(BF16), new3(h16["bv"])], axis=1), band_rows)

        km, vm = _kvup_call(ckv_all, kpe_all, w, tm if caches is None else tk)
        o_a = _causal_call(_mla_kernel, new3(h16["qm"]), km.reshape(b, tk_all, -1), vm.reshape(b, tk_all, -1),
                           [], [], tq, tk, past, "mla", chains=N_HEADS)
        o_b = _causal_call(_sb_kernel, new3(h16["sbq"]), keys("sbk", 2), keys("sbv", 3), [], [],
                           sb_tq, sb_tk, past, "sb")
        lam_init = 0.8 - 0.6 * math.exp(-0.3 * l)
        o_c = _causal_call(functools.partial(_diff_kernel, lam_init=lam_init), new3(h16["dq"]),
                           keys("dk", 4), diff_v, [w["lam"], w["subln"]],
                           [_const_spec((4, DIFF_HALF)), _const_spec((1, HEAD_DIM))], tq, tk, past, "diff",
                           chains=2 * N_HEADS)
        o_d = _band_call(new3(h16["bq"]), band_k, band_v, w["rel"], band_tq, band_win, past)
        mix = [o.reshape(n, GROUP_WIDTH) for o in (o_a, o_b, o_c, o_d)]
        h = _post_call(h, mix, w, tm, alpha)

        keep = min(BAND_ROWS, t)
        rows = (new3(f32["ckv"]), new3(f32["kpe"])[:, :, :MLA_ROPE],
                new3(f32["sbk"]), new3(f32["sbv"]), new3(f32["dk"]), new3(f32["dv"]),
                new3(f32["bk"])[:, t - keep:], new3(f32["bv"])[:, t - keep:])
        for i, (lst, arr) in enumerate(zip(new_rows, rows)):
            lst.append(arr if i < 2 else arr.reshape(b, arr.shape[1], N_HEADS, HEAD_DIM))
    return h.reshape(b, t, d), [jnp.stack(lst) for lst in new_rows]


def kernel(x_prompt, x_sample, cache_mla_ckv, cache_mla_kpe, cache_sb_k, cache_sb_v, cache_diff_k, cache_diff_v, cache_band_k, cache_band_v, ln_in_g, ln_in_b, w_in, mla_q_norm, mla_w_uq, mla_kv_norm, mla_w_uk, mla_w_uv, diff_lam_q1, diff_lam_k1, diff_lam_q2, diff_lam_k2, diff_subln, band_rel_bias, w_out, ln1_g, ln1_b, w_router_group, b_router_group, w_router_expert, b_router_expert, w_exp_gate, w_exp_up, w_exp_down, ln2_g, ln2_b):
    P = {
        "ln_in_g": ln_in_g, "ln_in_b": ln_in_b, "w_in": w_in,
        "mla_q_norm": mla_q_norm, "mla_w_uq": mla_w_uq, "mla_kv_norm": mla_kv_norm,
        "mla_w_uk": mla_w_uk, "mla_w_uv": mla_w_uv,
        "diff_lam_q1": diff_lam_q1, "diff_lam_k1": diff_lam_k1,
        "diff_lam_q2": diff_lam_q2, "diff_lam_k2": diff_lam_k2, "diff_subln": diff_subln,
        "band_rel_bias": band_rel_bias, "w_out": w_out, "ln1_g": ln1_g, "ln1_b": ln1_b,
        "w_router_group": w_router_group, "b_router_group": b_router_group,
        "w_router_expert": w_router_expert, "b_router_expert": b_router_expert,
        "w_exp_gate": w_exp_gate, "w_exp_up": w_exp_up, "w_exp_down": w_exp_down,
        "ln2_g": ln2_g, "ln2_b": ln2_b,
    }
    weights = [_layer_weights(l, P) for l in range(w_in.shape[0])]
    y_prompt, st_p = _trunk(x_prompt, None, P, weights)
    caches = (cache_mla_ckv, cache_mla_kpe, cache_sb_k, cache_sb_v,
              cache_diff_k, cache_diff_v, cache_band_k, cache_band_v)
    y_sample, st_s = _trunk(x_sample, caches, P, weights)
    return (y_prompt, y_sample, *st_p, *st_s)
```

```python
import functools
import math

import jax
import jax.numpy as jnp
from jax import lax
from jax.experimental import pallas as pl
from jax.experimental.pallas import tpu as pltpu

F32 = jnp.float32
BF16 = jnp.bfloat16

CHUNK = 64
HEAD_DIM = 64
N_HEADS = 4
GROUP_WIDTH = N_HEADS * HEAD_DIM
PAIR = 2 * HEAD_DIM
MLA_Q_LORA = 384
MLA_KV_LORA = 256
MLA_NOPE = 64
MLA_ROPE = 32
MLA_QBLK = 256
DIFF_HALF = 32
BAND_ROWS = 512
REL_CLIP = 128
N_GROUPS = 4
EXPERTS_PER_GROUP = 4
N_EXPERTS = N_GROUPS * EXPERTS_PER_GROUP
D_EXPERT = 128
ROPE_THETA = 10000.0
LN_EPS = 1e-5
NEG = -1e30
LANES = 128
SB_DEAD = -104.0
V_WIDE = N_HEADS * LANES
KPE_COL = MLA_Q_LORA + MLA_KV_LORA + 9 * GROUP_WIDTH
D_IN_PAD = KPE_COL + LANES
MLA_ONE_LANE = PAIR + MLA_ROPE
EXP_GUARD = 60.0
LOG2E = 1.4426950408889634
VMEM_LIMIT = 56 * 1024 * 1024

_NT = (((1,), (1,)), ((), ()))


def _params(*sem):
    return pltpu.CompilerParams(dimension_semantics=sem, vmem_limit_bytes=VMEM_LIMIT)


def _dot(a, b):
    return jnp.dot(a, b, preferred_element_type=F32)


def _dot_nt(a, b):
    return lax.dot_general(a, b, _NT, preferred_element_type=F32)


def _layer_norm(x, g, b):
    mu = jnp.mean(x, axis=-1, keepdims=True)
    var = jnp.mean(jnp.square(x - mu), axis=-1, keepdims=True)
    return (x - mu) * lax.rsqrt(var + LN_EPS) * g + b


def _rms_norm(x, w, eps):
    return x * lax.rsqrt(jnp.mean(jnp.square(x), axis=-1, keepdims=True) + eps) * w


def _rope(x, cos, sin_signed):
    n = x.shape[-1]
    lane = lax.broadcasted_iota(jnp.int32, x.shape, 1)
    partner = jnp.where((lane & 16) == 0, pltpu.roll(x, n - 16, 1), pltpu.roll(x, 16, 1))
    return x * cos + partner * sin_signed


def _ones_row(shape):
    row = lax.broadcasted_iota(jnp.int32, shape, 0)
    return jnp.where((row & (LANES - 1)) == HEAD_DIM, 1.0, 0.0)


def _head_lanes(shape, lo, width):
    lane = lax.broadcasted_iota(jnp.int32, shape, 1)
    return (lane >= lo) & (lane < lo + width)


def _row_spec(tm, width):
    return pl.BlockSpec((tm, width), lambda i: (i, 0))


def _const_spec(shape):
    return pl.BlockSpec(shape, lambda *_: (0,) * len(shape))


def _resident_spec(rows, width):
    return pl.BlockSpec((1, rows, width), lambda b, i: (b, 0, 0), pipeline_mode=pl.Buffered(1))


def _ln_kernel(x_ref, g_ref, b_ref, o_ref):
    o_ref[...] = _layer_norm(x_ref[...], g_ref[...], b_ref[...])


def _ln_call(x, g, b, tm):
    n, d = x.shape
    return pl.pallas_call(
        _ln_kernel, grid=(n // tm,),
        in_specs=[_row_spec(tm, d), _const_spec((1, d)), _const_spec((1, d))],
        out_specs=_row_spec(tm, d),
        out_shape=jax.ShapeDtypeStruct((n, d), F32),
        compiler_params=_params("parallel"), name="ln_in",
    )(x, g.reshape(1, d), b.reshape(1, d))


_PRE_F32 = ("ckv", "kpe", "sbk", "sbv", "dk", "dv", "bk", "bv")
_PRE_BF16 = ("sbq", "sbk", "sbv", "dq", "dk", "dv", "bq", "bk", "bv")


def _pre_kernel(x_ref, cos_ref, sin_ref, win_ref, qn_ref, wuq_ref, kvn_ref, wdvt_ref,
                ckv_o, kpe_o, sbk_o, sbv_o, dk_o, dv_o, bk_o, bv_o,
                qm_o, sbq_h, sbk_h, sbv_h, dq_h, dk_h, dv_h, bq_h, bk_h, bv_h):
    xb = x_ref[...].astype(BF16)
    cos = cos_ref[...]
    sin = sin_ref[...]
    cos_l, sin_l = cos[:, :LANES], sin[:, :LANES]

    def proj(seg, width=GROUP_WIDTH):
        return _dot(xb, win_ref[:, seg:seg + width])

    mla_scale = (MLA_NOPE + MLA_ROPE) ** -0.5 * LOG2E
    cq =_rms_norm(proj(0, MLA_Q_LORA), qn_ref[...], 1e-6)
    q = _dot(cq.astype(BF16), wuq_ref[...])
    for h in range(N_HEADS):
        lo = h * MLA_QBLK
        qm_o[:, lo:lo + LANES] = (q[:, lo:lo + LANES] * mla_scale).astype(BF16)
        qr = _rope(q[:, lo + LANES:lo + MLA_QBLK], cos_l, sin_l)
        qm_o[:, lo + LANES:lo + MLA_QBLK] = (qr * mla_scale).astype(BF16)
    seg = MLA_Q_LORA
    ckv_o[...] = _rms_norm(proj(seg), kvn_ref[...], 1e-6)
    seg += MLA_KV_LORA
    kpe_o[...] = _rope(proj(KPE_COL, LANES), cos_l, sin_l)

    sbq_h[...] = (proj(seg) * HEAD_DIM ** -0.5).astype(BF16)
    k = proj(seg + GROUP_WIDTH)
    sbk_o[...] = k
    sbk_h[...] = k.astype(BF16)
    v = proj(seg + 2 * GROUP_WIDTH)
    sbv_o[...] = v
    sbv_h[...] = v.astype(BF16)
    seg += 3 * GROUP_WIDTH
    dq_h[...] = (_rope(proj(seg), cos, sin) * (DIFF_HALF ** -0.5 * LOG2E)).astype(BF16)
    k = _rope(proj(seg + GROUP_WIDTH), cos, sin)
    dk_o[...] = k
    dk_h[...] = k.astype(BF16)
    dv_o[...] = proj(seg + 2 * GROUP_WIDTH)
    dv_h[0] = (_dot_nt(wdvt_ref[...], xb) + _ones_row(dv_h.shape[1:])).astype(BF16)
    seg += 3 * GROUP_WIDTH
    bq_h[...] = (proj(seg) * HEAD_DIM ** -0.5).astype(BF16)
    k = proj(seg + GROUP_WIDTH)
    bk_o[...] = k
    bk_h[...] = k.astype(BF16)
    v = proj(seg + 2 * GROUP_WIDTH)
    bv_o[...] = v
    bv_h[...] = v.astype(BF16)


def _pre_call(x, cos, sin, w, t, tm):
    n, d = x.shape
    nt = t // tm
    tab = pl.BlockSpec((tm, GROUP_WIDTH), lambda i: (i % nt, 0))
    widths_f32 = [GROUP_WIDTH, LANES] + [GROUP_WIDTH] * 6
    widths_h = [N_HEADS * MLA_QBLK] + [GROUP_WIDTH] * 9
    out_shape = ([jax.ShapeDtypeStruct((n, wd), F32) for wd in widths_f32]
                 + [jax.ShapeDtypeStruct((n, wd), BF16) for wd in widths_h])
    out_specs = [_row_spec(tm, wd) for wd in widths_f32 + widths_h]
    dv = len(widths_f32) + 1 + _PRE_BF16.index("dv")
    out_shape[dv] = jax.ShapeDtypeStruct((n // t, V_WIDE, t), BF16)
    out_specs[dv] = pl.BlockSpec((1, V_WIDE, tm), lambda i: (i // nt, 0, i % nt))
    outs = pl.pallas_call(
        _pre_kernel, grid=(n // tm,),
        in_specs=[_row_spec(tm, d), tab, tab,
                  _const_spec((d, D_IN_PAD)), _const_spec((1, MLA_Q_LORA)),
                  _const_spec((MLA_Q_LORA, N_HEADS * MLA_QBLK)), _const_spec((1, MLA_KV_LORA)),
                  _const_spec((V_WIDE, d))],
        out_specs=out_specs,
        out_shape=out_shape,
        compiler_params=_params("parallel"), name="pre",
    )(x, cos, sin, w["w_in"], w["q_norm"], w["w_uq"], w["kv_norm"], w["w_dv_t"])
    f32 = dict(zip(_PRE_F32, outs[:8]))
    h16 = dict(zip(("qm",) + _PRE_BF16, outs[8:]))
    return f32, h16


def _kvup_kernel(ckv_ref, kpe_ref, wuk_ref, wuv_ref, km_o, vm_o):
    cb = ckv_ref[...].astype(BF16)
    kn = _dot(cb, wuk_ref[...]).astype(BF16)
    lane = lax.broadcasted_iota(jnp.int32, kpe_ref.shape, 1)
    kpe = jnp.where(lane == MLA_ROPE, 1.0, kpe_ref[...]).astype(BF16)
    for p in range(2):
        km_o[:, p * MLA_QBLK:p * MLA_QBLK + PAIR] = kn[:, p * PAIR:(p + 1) * PAIR]
        km_o[:, p * MLA_QBLK + PAIR:(p + 1) * MLA_QBLK] = kpe
    vm_o[0] = (_dot_nt(wuv_ref[...], cb) + _ones_row(vm_o.shape[1:])).astype(BF16)


def _kvup_call(ckv, kpe, w, tm, rows):
    n = ckv.shape[0]
    nt = rows // tm
    return pl.pallas_call(
        _kvup_kernel, grid=(n // tm,),
        in_specs=[_row_spec(tm, MLA_KV_LORA), _row_spec(tm, LANES),
                  _const_spec((MLA_KV_LORA, GROUP_WIDTH)), _const_spec((V_WIDE, MLA_KV_LORA))],
        out_specs=[_row_spec(tm, 2 * MLA_QBLK), pl.BlockSpec((1, V_WIDE, tm), lambda i: (i // nt, 0, i % nt))],
        out_shape=[jax.ShapeDtypeStruct((n, 2 * MLA_QBLK), BF16),
                   jax.ShapeDtypeStruct((n // rows, V_WIDE, rows), BF16)],
        compiler_params=_params("parallel"), name="kvup",
    )(ckv, kpe, w["w_uk"], w["w_uv"])


def _chunk_mask(i, jd, tq, tk, off):
    qpos = off + i * tq + lax.broadcasted_iota(jnp.int32, (tq, tk), 0)
    kpos = jd * tk + lax.broadcasted_iota(jnp.int32, (tq, tk), 1)
    return (kpos // CHUNK) <= (qpos // CHUNK)


def _fold_max(d):
    m = d[:, :LANES]
    for c in range(1, d.shape[1] // LANES):
        m = jnp.maximum(m, d[:, c * LANES:(c + 1) * LANES])
    return jnp.max(m.reshape(m.shape[0] // 8, 8, m.shape[1]), axis=0)


def _softmax_chains(chains, operands, i, jd, tq, tk, off, qs_ref, acc_ref):
    qpos = off + i * tq + lax.broadcasted_iota(jnp.int32, (tk, tq), 1)
    kpos = jd * tk + lax.broadcasted_iota(jnp.int32, (tk, tq), 0)
    mask_t = (kpos // CHUNK) <= (qpos // CHUNK)
    own = pl.multiple_of(off + i * tq, tq)
    own_mask = (lax.broadcasted_iota(jnp.int32, (tq, tq), 1) // CHUNK
                <= lax.broadcasted_iota(jnp.int32, (tq, tq), 0) // CHUNK)
    for n in range(chains):
        q, k, _ = operands(n, own, tq)
        m0 = jnp.max(jnp.where(own_mask, _dot_nt(q, k), NEG), axis=-1, keepdims=True)
        lane = lax.broadcasted_iota(jnp.int32, q.shape, 1)
        qs_ref[n] = jnp.where(lane == operands(n, None), (-m0).astype(BF16), q)
    acc_ref[...] = jnp.zeros(acc_ref.shape, F32)

    def block(j, top, diagonal):
        start = pl.multiple_of(j * tk, tk)
        d_next = _dot_nt(operands(0, start)[1], qs_ref[0])
        for n in range(chains):
            d = d_next
            if n + 1 < chains:
                d_next = _dot_nt(operands(n + 1, start)[1], qs_ref[n + 1])
            if diagonal:
                d = jnp.where(mask_t, d, NEG)
            top = jnp.maximum(top, _fold_max(d))
            acc_ref[n] += _dot(operands(n, start)[2], jnp.exp2(d).astype(BF16))
        return top

    top = lax.fori_loop(0, jd, lambda j, top: block(j, top, False),
                        jnp.full((8, min(tq, LANES)), NEG, F32))
    top = block(jd, top, True)

    @pl.when(jnp.max(top) > EXP_GUARD)
    def _():
        def safe(j, states, diagonal):
            start = pl.multiple_of(j * tk, tk)
            new = []
            for n in range(chains):
                q, k, vt = operands(n, start)
                s = _dot_nt(k, q)
                if diagonal:
                    s = jnp.where(mask_t, s, NEG)
                m, acc = states[n]
                m_new = jnp.maximum(m, jnp.max(s, axis=0, keepdims=True))
                new.append((m_new, jnp.exp2(m - m_new) * acc + _dot(vt, jnp.exp2(s - m_new).astype(BF16))))
            return tuple(new)

        init = tuple((jnp.full((1, tq), NEG, F32), jnp.zeros((LANES, tq), F32)) for _ in range(chains))
        states = safe(jd, lax.fori_loop(0, jd, lambda j, st: safe(j, st, False), init), True)
        for n in range(chains):
            acc_ref[n] = states[n][1]


def _flash_scratch(chains, tq):
    return [pltpu.VMEM((chains, tq, MLA_QBLK), BF16), pltpu.VMEM((chains, LANES, tq), F32)]


def _mla_kernel(q_ref, k_ref, v_ref, o_ref, qs_ref, acc_ref, *, tq, tk, off):
    i = pl.program_id(1)
    jd = (off + i * tq) // tk

    def operands(h, start, rows=tk):
        if start is None:
            return MLA_ONE_LANE
        p = h // 2
        return (q_ref[0, :, h * MLA_QBLK:(h + 1) * MLA_QBLK],
                k_ref[0, pl.ds(start, rows), p * MLA_QBLK:(p + 1) * MLA_QBLK],
                v_ref[0, h * LANES:(h + 1) * LANES, pl.ds(start, rows)])

    _softmax_chains(N_HEADS, operands, i, jd, tq, tk, off, qs_ref, acc_ref)
    for p in range(2):
        pair = []
        for h in (2 * p, 2 * p + 1):
            acc = acc_ref[h]
            pair.append(acc[:HEAD_DIM] / acc[HEAD_DIM:HEAD_DIM + 1])
        o_ref[0, :, p * PAIR:(p + 1) * PAIR] = jnp.concatenate(pair, axis=0).T.astype(BF16)


def _diff_kernel(q_ref, k_ref, v_ref, lam_ref, subln_ref, o_ref, qs_ref, acc_ref, *, tq, tk, off, lam_init):
    i = pl.program_id(1)
    jd = (off + i * tq) // tk

    def operands(n, start, rows=tk):
        if start is None:
            return PAIR
        h, half = n // 2, n % 2
        p, hh = h // 2, h % 2
        qp = q_ref[0, :, p * PAIR:(p + 1) * PAIR]
        sel = _head_lanes((tq, PAIR), hh * HEAD_DIM + half * DIFF_HALF, DIFF_HALF)
        zero = jnp.zeros_like(qp)
        one = jnp.where(lax.broadcasted_iota(jnp.int32, (rows, PAIR), 1) == 0, 1.0, 0.0).astype(BF16)
        return (jnp.concatenate([jnp.where(sel, qp, zero), zero], axis=-1),
                jnp.concatenate([k_ref[0, pl.ds(start, rows), p * PAIR:(p + 1) * PAIR], one], axis=-1),
                v_ref[0, h * LANES:(h + 1) * LANES, pl.ds(start, rows)])

    _softmax_chains(2 * N_HEADS, operands, i, jd, tq, tk, off, qs_ref, acc_ref)

    lp = lam_ref[...]
    lam = (jnp.exp(jnp.sum(lp[0:1] * lp[1:2], axis=-1, keepdims=True))
           - jnp.exp(jnp.sum(lp[2:3] * lp[3:4], axis=-1, keepdims=True)) + lam_init)
    gain = subln_ref[...]
    for p in range(2):
        pair = []
        for h in (2 * p, 2 * p + 1):
            a1, a2 = acc_ref[2 * h], acc_ref[2 * h + 1]
            o = (a1[:HEAD_DIM] / a1[HEAD_DIM:HEAD_DIM + 1]
                 - lam * (a2[:HEAD_DIM] / a2[HEAD_DIM:HEAD_DIM + 1]))
            ms = jnp.mean(o * o, axis=0, keepdims=True)
            pair.append(o * lax.rsqrt(ms + 1e-5) * gain * (1.0 - lam_init))
        o_ref[0, :, p * PAIR:(p + 1) * PAIR] = jnp.concatenate(pair, axis=0).T.astype(BF16)


def _sb_scratch(tq):
    return [pltpu.VMEM((N_HEADS, tq, PAIR), BF16), pltpu.VMEM((N_HEADS, tq, 1), F32),
            pltpu.VMEM((N_HEADS, tq, PAIR), F32)]


def _sb_kernel(q_ref, k_ref, v_ref, o_ref, qs_ref, later_ref, acc_ref, *, tq, tk, off):
    i = pl.program_id(1)
    jd = (off + i * tq) // tk
    row = lax.broadcasted_iota(jnp.int32, (tk, tk), 0)
    col = lax.broadcasted_iota(jnp.int32, (tk, tk), 1)
    after = jnp.where(row > col, 1.0, 0.0).astype(BF16)
    qpos = off + i * tq + lax.broadcasted_iota(jnp.int32, (tq, tk), 0)
    kcol = lax.broadcasted_iota(jnp.int32, (tq, tk), 1)
    first = _head_lanes((tq, PAIR), 0, HEAD_DIM)

    for h in range(N_HEADS):
        qp = q_ref[0, :, (h // 2) * PAIR:(h // 2 + 1) * PAIR]
        qs_ref[h] = jnp.where(_head_lanes((tq, PAIR), (h % 2) * HEAD_DIM, HEAD_DIM), qp, jnp.zeros_like(qp))
    later_ref[...] = jnp.zeros(later_ref.shape, F32)
    acc_ref[...] = jnp.zeros(acc_ref.shape, F32)

    def block(j, diagonal):
        start = pl.multiple_of(j * tk, tk)
        for h in range(N_HEADS):
            p = h // 2
            kp = k_ref[0, pl.ds(start, tk), p * PAIR:(p + 1) * PAIR]
            vp = v_ref[0, pl.ds(start, tk), p * PAIR:(p + 1) * PAIR]
            z = _dot_nt(qs_ref[h], kp)
            t = jnp.log(1.0 + jnp.exp(-jnp.abs(z)))
            log_keep = -(jnp.maximum(z, 0.0) + t)
            log_beta = jnp.minimum(z, 0.0) - t
            if diagonal:
                earlier = (start + kcol) < qpos
                log_keep = jnp.where(earlier, log_keep, 0.0)
            hi = log_keep.astype(BF16)
            lo = (log_keep - hi.astype(F32)).astype(BF16)
            within = _dot(hi, after) + _dot(lo, after)
            w = jnp.exp(log_beta + within + later_ref[h])
            if diagonal:
                w = jnp.where(earlier, w, 0.0)
            acc_ref[h] += _dot(w.astype(BF16), vp)
            later_ref[h] += jnp.sum(log_keep, axis=-1, keepdims=True)
        return (jnp.max(later_ref[...]) > SB_DEAD).astype(jnp.int32)

    def body(c):
        j, _ = c
        return j - 1, block(j, False)

    lax.while_loop(lambda c: (c[0] >= 0) & (c[1] > 0), body, (jd - 1, block(jd, True)))
    for p in range(2):
        o_ref[0, :, p * PAIR:(p + 1) * PAIR] = jnp.where(first, acc_ref[2 * p], acc_ref[2 * p + 1]).astype(BF16)


def _causal_call(kernel, q, k, v, extra, extra_specs, tq, tk, off, name, scratch):
    b, t, qw = q.shape
    tk_all = k.shape[1]
    assert off % tk == 0 and tk % tq == 0 and t % tq == 0 and tk_all % tk == 0
    assert off + t <= tk_all
    return pl.pallas_call(
        functools.partial(kernel, tq=tq, tk=tk, off=off), grid=(b, t // tq),
        in_specs=[pl.BlockSpec((1, tq, qw), lambda b_, i: (b_, i, 0)),
                  _resident_spec(tk_all, k.shape[2]), _resident_spec(v.shape[1], v.shape[2])] + extra_specs,
        out_specs=pl.BlockSpec((1, tq, GROUP_WIDTH), lambda b_, i: (b_, i, 0)),
        out_shape=jax.ShapeDtypeStruct((b, t, GROUP_WIDTH), BF16),
        scratch_shapes=scratch,
        compiler_params=_params("parallel", "arbitrary"), name=name,
    )(q, k, v, *extra)


def _band_kernel(tab_ref, q_ref, k_ref, v_ref, o_ref, bias_ref, *, tq, win, pos0):
    b = pl.program_id(0)
    i = pl.program_id(1)

    @pl.when((b == 0) & (i == 0))
    def _():
        rows = 8

        def fill(c, carry):
            r0 = pl.multiple_of(c * rows, rows)
            r = r0 + lax.broadcasted_iota(jnp.int32, (rows, win), 0)
            u = lax.broadcasted_iota(jnp.int32, (rows, win), 1)
            rel = jnp.clip(r - u + BAND_ROWS, -REL_CLIP, REL_CLIP) + REL_CLIP
            dchunk = r // CHUNK - u // CHUNK + BAND_ROWS // CHUNK
            ok = (dchunk >= 0) & (dchunk <= BAND_ROWS // CHUNK)

            def pick(d, vals):
                hit = rel == d
                return tuple(jnp.where(hit, tab_ref[h, d], vals[h]) for h in range(N_HEADS))

            vals = lax.fori_loop(0, 2 * REL_CLIP + 1, pick,
                                 tuple(jnp.zeros((rows, win), F32) for _ in range(N_HEADS)))
            for h in range(N_HEADS):
                bias_ref[h, pl.ds(r0, rows), :] = jnp.where(ok, vals[h], NEG)
            return carry

        lax.fori_loop(0, tq // rows, fill, 0)

    start = pl.multiple_of(i * tq, tq)
    u = lax.broadcasted_iota(jnp.int32, (tq, win), 1)
    exists = (pos0 + i * tq - BAND_ROWS + u) >= 0
    first = _head_lanes((tq, PAIR), 0, HEAD_DIM)
    for p in range(2):
        qp = q_ref[0, :, p * PAIR:(p + 1) * PAIR]
        kw = k_ref[0, pl.ds(start, win), p * PAIR:(p + 1) * PAIR]
        vw = v_ref[0, pl.ds(start, win), p * PAIR:(p + 1) * PAIR]
        outs = []
        for hh in range(2):
            qh = jnp.where(_head_lanes((tq, PAIR), hh * HEAD_DIM, HEAD_DIM), qp, jnp.zeros_like(qp))
            s = jnp.where(exists, _dot_nt(qh, kw) + bias_ref[2 * p + hh], NEG)
            e = jnp.exp(s - jnp.max(s, axis=-1, keepdims=True))
            outs.append(_dot(e.astype(BF16), vw) / jnp.sum(e, axis=-1, keepdims=True))
        o_ref[0, :, p * PAIR:(p + 1) * PAIR] = jnp.where(first, outs[0], outs[1]).astype(BF16)


def _band_call(q, k_pad, v_pad, rel_table, tq, win, pos0):
    b, t, _ = q.shape
    rows = k_pad.shape[1]
    assert t % tq == 0 and tq % CHUNK == 0 and win % LANES == 0 and win >= tq + BAND_ROWS
    assert rows >= t - tq + win and pos0 % CHUNK == 0
    return pl.pallas_call(
        functools.partial(_band_kernel, tq=tq, win=win, pos0=pos0), grid=(b, t // tq),
        in_specs=[pl.BlockSpec(memory_space=pltpu.SMEM),
                  pl.BlockSpec((1, tq, GROUP_WIDTH), lambda b_, i: (b_, i, 0)),
                  _resident_spec(rows, GROUP_WIDTH), _resident_spec(rows, GROUP_WIDTH)],
        out_specs=pl.BlockSpec((1, tq, GROUP_WIDTH), lambda b_, i: (b_, i, 0)),
        out_shape=jax.ShapeDtypeStruct((b, t, GROUP_WIDTH), BF16),
        scratch_shapes=[pltpu.VMEM((N_HEADS, tq, win), F32)],
        compiler_params=_params("arbitrary", "arbitrary"), name="band",
    )(rel_table, q, k_pad, v_pad)


def _post_kernel(h_ref, oa_ref, ob_ref, oc_ref, od_ref, wout_ref, g1_ref, b1_ref,
                 wr_hi_ref, wr_lo_ref, br_ref, wg_ref, wu_ref, wd_ref, g2_ref, b2_ref, o_ref, *, alpha):
    tm = h_ref.shape[0]
    y = jnp.zeros(h_ref.shape, F32)
    for g, ref in enumerate((oa_ref, ob_ref, oc_ref, od_ref)):
        y = y + _dot(ref[...], wout_ref[g * GROUP_WIDTH:(g + 1) * GROUP_WIDTH, :])
    x = _layer_norm(alpha * h_ref[...] + y, g1_ref[...], b1_ref[...])

    x_hi = x.astype(BF16)
    x_lo = (x - x_hi.astype(F32)).astype(BF16)
    logit = (_dot(x_hi, wr_hi_ref[...]) + _dot(x_lo, wr_hi_ref[...]) + _dot(x_hi, wr_lo_ref[...])
             + br_ref[...])
    lane = lax.broadcasted_iota(jnp.int32, (tm, LANES), 1).astype(F32)
    ninf = -jnp.inf

    def lane_max(a):
        return jnp.max(a, axis=-1, keepdims=True)

    def first_lane(hit):
        return jnp.min(jnp.where(hit, lane, float(LANES)), axis=-1, keepdims=True)

    gl = jnp.where(lane < N_GROUPS, logit, ninf)
    g_max = lane_max(gl)
    g_idx = first_lane(gl == g_max)
    g_weight = 1.0 / jnp.sum(jnp.exp(gl - g_max), axis=-1, keepdims=True)
    e_lo = N_GROUPS + EXPERTS_PER_GROUP * g_idx
    el = jnp.where((lane >= e_lo) & (lane < e_lo + EXPERTS_PER_GROUP), logit, ninf)
    v1 = lane_max(el)
    i1 = first_lane(el == v1)
    el2 = jnp.where(lane == i1, ninf, el)
    v2 = lane_max(el2)
    i2 = first_lane(el2 == v2)
    e2 = jnp.exp(v2 - v1)
    w1 = g_weight / (1.0 + e2)
    w2 = g_weight * e2 / (1.0 + e2)
    comb = jnp.where(lane == i1, w1, 0.0) + jnp.where(lane == i2, w2, 0.0)

    xb = x_hi
    gate = _dot(xb, wg_ref[...])
    up = _dot(xb, wu_ref[...])
    hidden = []
    for e in range(N_EXPERTS):
        c_e = jnp.sum(jnp.where(lane == float(N_GROUPS + e), comb, 0.0), axis=-1, keepdims=True)
        ge = gate[:, e * D_EXPERT:(e + 1) * D_EXPERT]
        ue = up[:, e * D_EXPERT:(e + 1) * D_EXPERT]
        hidden.append(((ge * (1.0 / (1.0 + jnp.exp(-ge)))) * ue * c_e).astype(BF16))
    f = _dot(jnp.concatenate(hidden, axis=-1), wd_ref[...])
    o_ref[...] = _layer_norm(alpha * x + f, g2_ref[...], b2_ref[...])


def _post_call(h, mix, w, tm, alpha):
    n, d = h.shape
    de = N_EXPERTS * D_EXPERT
    return pl.pallas_call(
        functools.partial(_post_kernel, alpha=alpha), grid=(n // tm,),
        in_specs=[_row_spec(tm, d)] + [_row_spec(tm, GROUP_WIDTH)] * 4
        + [_const_spec((d, d)), _const_spec((1, d)), _const_spec((1, d)),
           _const_spec((d, LANES)), _const_spec((d, LANES)), _const_spec((1, LANES)),
           _const_spec((d, de)), _const_spec((d, de)), _const_spec((de, d)),
           _const_spec((1, d)), _const_spec((1, d))],
        out_specs=_row_spec(tm, d),
        out_shape=jax.ShapeDtypeStruct((n, d), F32),
        compiler_params=_params("parallel"), name="post",
    )(h, *mix, w["w_out"], w["ln1_g"], w["ln1_b"], w["wr_hi"], w["wr_lo"], w["b_r"],
      w["w_gate"], w["w_up"], w["w_down"], w["ln2_g"], w["ln2_b"])


def _wide_columns(w):
    rows = w.shape[0]
    w = w.reshape(rows, N_HEADS, HEAD_DIM)
    return jnp.concatenate([w, jnp.zeros_like(w)], axis=-1).reshape(rows, V_WIDE)


def _wide_values_t(v):
    one = jnp.ones(v.shape[:-1] + (1,), v.dtype)
    zero = jnp.zeros(v.shape[:-1] + (LANES - HEAD_DIM - 1,), v.dtype)
    wide = jnp.concatenate([v, one, zero], axis=-1).reshape(v.shape[0], v.shape[1], V_WIDE)
    return wide.transpose(0, 2, 1)


def _layer_weights(l, P):
    d = P["w_in"].shape[1]
    w_in = P["w_in"][l]
    kpe_lo = MLA_Q_LORA + MLA_KV_LORA
    dv_lo = kpe_lo + MLA_ROPE + 5 * GROUP_WIDTH
    w_in_pad = jnp.concatenate(
        [w_in[:, :kpe_lo], w_in[:, kpe_lo + MLA_ROPE:], w_in[:, kpe_lo:kpe_lo + MLA_ROPE],
         jnp.zeros((d, LANES - MLA_ROPE), F32)], axis=1)
    w_uq = P["mla_w_uq"][l]
    blocks = []
    for h in range(N_HEADS):
        src = h * (MLA_NOPE + MLA_ROPE)
        nope = w_uq[:, src:src + MLA_NOPE]
        zero = jnp.zeros_like(nope)
        blocks += ([nope, zero] if h % 2 == 0 else [zero, nope])
        blocks += [w_uq[:, src + MLA_NOPE:src + MLA_NOPE + MLA_ROPE],
                   jnp.zeros((MLA_Q_LORA, LANES - MLA_ROPE), F32)]
    w_r = jnp.concatenate([P["w_router_group"][l], P["w_router_expert"][l],
                           jnp.zeros((d, LANES - N_GROUPS - N_EXPERTS), F32)], axis=1)
    wr_hi = w_r.astype(BF16)
    b_r = jnp.concatenate([P["b_router_group"][l], P["b_router_expert"][l],
                           jnp.zeros((LANES - N_GROUPS - N_EXPERTS,), F32)]).reshape(1, LANES)
    de = N_EXPERTS * D_EXPERT
    return {
        "w_in": w_in_pad.astype(BF16),
        "q_norm": P["mla_q_norm"][l].reshape(1, -1),
        "w_uq": jnp.concatenate(blocks, axis=1).astype(BF16),
        "kv_norm": P["mla_kv_norm"][l].reshape(1, -1),
        "w_uk": P["mla_w_uk"][l].astype(BF16),
        "w_uv": _wide_columns(P["mla_w_uv"][l]).T.astype(BF16),
        "w_dv_t": _wide_columns(w_in[:, dv_lo:dv_lo + GROUP_WIDTH]).T.astype(BF16),
        "lam": jnp.stack([P["diff_lam_q1"][l], P["diff_lam_k1"][l],
                          P["diff_lam_q2"][l], P["diff_lam_k2"][l]]),
        "subln": P["diff_subln"][l].reshape(HEAD_DIM, 1),
        "rel": P["band_rel_bias"][l],
        "w_out": P["w_out"][l].astype(BF16),
        "ln1_g": P["ln1_g"][l].reshape(1, -1), "ln1_b": P["ln1_b"][l].reshape(1, -1),
        "wr_hi": wr_hi, "wr_lo": (w_r - wr_hi.astype(F32)).astype(BF16), "b_r": b_r,
        "w_gate": P["w_exp_gate"][l].transpose(2, 0, 1, 3).reshape(d, de).astype(BF16),
        "w_up": P["w_exp_up"][l].transpose(2, 0, 1, 3).reshape(d, de).astype(BF16),
        "w_down": P["w_exp_down"][l].reshape(de, d).astype(BF16),
        "ln2_g": P["ln2_g"][l].reshape(1, -1), "ln2_b": P["ln2_b"][l].reshape(1, -1),
    }


def _rope_tables(pos):
    inv_freq = ROPE_THETA ** (-jnp.arange(0, MLA_ROPE, 2, dtype=F32) / MLA_ROPE)
    ang = pos.astype(F32)[:, None] * inv_freq[None, :]
    cos, sin = jnp.cos(ang), jnp.sin(ang)
    reps = GROUP_WIDTH // MLA_ROPE
    return (jnp.tile(jnp.concatenate([cos, cos], axis=1), (1, reps)),
            jnp.tile(jnp.concatenate([-sin, sin], axis=1), (1, reps)))


def _flat_heads(a):
    return a.reshape(a.shape[0], a.shape[1], GROUP_WIDTH)


def _pad_rows(a, rows):
    return jnp.pad(a, ((0, 0), (0, rows - a.shape[1]), (0, 0)))


def _trunk(x, caches, P, weights):
    b, t, d = x.shape
    depth = len(weights)
    alpha = (2 * depth) ** 0.25
    n = b * t
    past = 0 if caches is None else caches[0].shape[2]
    if caches is None:
        tm, tq, tk, band_tq = 256, 512, 1024, 256
        sb_tq = sb_tk = 256
        tk_all = t
    else:
        tm, tq, band_tq = t, t, t
        tk = next(c for c in (512, 256, 2 * t) if past % c == 0)
        sb_tq, sb_tk = t, 2 * t
        tk_all = -(-(past + t) // tk) * tk
        assert caches[6].shape[2] == BAND_ROWS
    assert t >= BAND_ROWS or caches is not None
    band_win = -(-(band_tq + BAND_ROWS) // LANES) * LANES
    band_rows = t - band_tq + band_win
    cos, sin = _rope_tables(past + jnp.arange(t, dtype=jnp.int32))
    h = _ln_call(x.reshape(n, d), P["ln_in_g"], P["ln_in_b"], tm)
    new_rows = [[] for _ in range(8)]
    for l in range(depth):
        w = weights[l]
        f32, h16 = _pre_call(h, cos, sin, w, t, tm)

        def new3(a):
            return a.reshape(b, t, a.shape[-1])

        if caches is None:
            def keys(name, cache_idx):
                return new3(h16[name])
            diff_v = h16["dv"]
            ckv_all, kpe_all = f32["ckv"], f32["kpe"]
            band_k = jnp.pad(new3(h16["bk"]), ((0, 0), (BAND_ROWS, band_rows - BAND_ROWS - t), (0, 0)))
            band_v = jnp.pad(new3(h16["bv"]), ((0, 0), (BAND_ROWS, band_rows - BAND_ROWS - t), (0, 0)))
        else:
            c = [a[l] for a in caches]

            def keys(name, cache_idx):
                old = _flat_heads(c[cache_idx]).astype(BF16)
                return _pad_rows(jnp.concatenate([old, new3(h16[name])], axis=1), tk_all)
            diff_v = jnp.concatenate([_wide_values_t(c[5]).astype(BF16), h16["dv"]], axis=2)
            diff_v = jnp.pad(diff_v, ((0, 0), (0, 0), (0, tk_all - past - t)))
            ckv_all =_pad_rows(jnp.concatenate([c[0], new3(f32["ckv"])], axis=1), tk_all)
            ckv_all = ckv_all.reshape(b * tk_all, MLA_KV_LORA)
            kpe_old = jnp.pad(c[1], ((0, 0), (0, 0), (0, LANES - MLA_ROPE)))
            kpe_all = _pad_rows(jnp.concatenate([kpe_old, new3(f32["kpe"])], axis=1), tk_all)
            kpe_all = kpe_all.reshape(b * tk_all, LANES)
            band_k = _pad_rows(jnp.concatenate([_flat_heads(c[6]).astype(BF16), new3(h16["bk"])], axis=1), band_rows)
            band_v = _pad_rows(jnp.concatenate([_flat_heads(c[7]).astype(BF16), new3(h16["bv"])], axis=1), band_rows)

        km, vm = _kvup_call(ckv_all, kpe_all, w, tm if caches is None else tk, tk_all)
        o_a = _causal_call(_mla_kernel, new3(h16["qm"]), km.reshape(b, tk_all, -1), vm,
                           [], [], tq, tk, past, "mla", _flash_scratch(N_HEADS, tq))
        o_b = _causal_call(_sb_kernel, new3(h16["sbq"]), keys("sbk", 2), keys("sbv", 3), [], [],
                           sb_tq, sb_tk, past, "sb", _sb_scratch(sb_tq))
        lam_init = 0.8 - 0.6 * math.exp(-0.3 * l)
        o_c = _causal_call(functools.partial(_diff_kernel, lam_init=lam_init), new3(h16["dq"]),
                           keys("dk", 4), diff_v, [w["lam"], w["subln"]],
                           [_const_spec((4, DIFF_HALF)), _const_spec((HEAD_DIM, 1))], tq, tk, past, "diff",
                           _flash_scratch(2 * N_HEADS, tq))
        o_d = _band_call(new3(h16["bq"]), band_k, band_v, w["rel"], band_tq, band_win, past)
        mix = [o.reshape(n, GROUP_WIDTH) for o in (o_a, o_b, o_c, o_d)]
        h = _post_call(h, mix, w, tm, alpha)

        keep = min(BAND_ROWS, t)
        rows = (new3(f32["ckv"]), new3(f32["kpe"])[:, :, :MLA_ROPE],
                new3(f32["sbk"]), new3(f32["sbv"]), new3(f32["dk"]), new3(f32["dv"]),
                new3(f32["bk"])[:, t - keep:], new3(f32["bv"])[:, t - keep:])
        for i, (lst, arr) in enumerate(zip(new_rows, rows)):
            lst.append(arr if i < 2 else arr.reshape(b, arr.shape[1], N_HEADS, HEAD_DIM))
    return h.reshape(b, t, d), [jnp.stack(lst) for lst in new_rows]


def kernel(x_prompt, x_sample, cache_mla_ckv, cache_mla_kpe, cache_sb_k, cache_sb_v, cache_diff_k, cache_diff_v, cache_band_k, cache_band_v, ln_in_g, ln_in_b, w_in, mla_q_norm, mla_w_uq, mla_kv_norm, mla_w_uk, mla_w_uv, diff_lam_q1, diff_lam_k1, diff_lam_q2, diff_lam_k2, diff_subln, band_rel_bias, w_out, ln1_g, ln1_b, w_router_group, b_router_group, w_router_expert, b_router_expert, w_exp_gate, w_exp_up, w_exp_down, ln2_g, ln2_b):
    P = {
        "ln_in_g": ln_in_g, "ln_in_b": ln_in_b, "w_in": w_in,
        "mla_q_norm": mla_q_norm, "mla_w_uq": mla_w_uq, "mla_kv_norm": mla_kv_norm,
        "mla_w_uk": mla_w_uk, "mla_w_uv": mla_w_uv,
        "diff_lam_q1": diff_lam_q1, "diff_lam_k1": diff_lam_k1,
        "diff_lam_q2": diff_lam_q2, "diff_lam_k2": diff_lam_k2, "diff_subln": diff_subln,
        "band_rel_bias": band_rel_bias, "w_out": w_out, "ln1_g": ln1_g, "ln1_b": ln1_b,
        "w_router_group": w_router_group, "b_router_group": b_router_group,
        "w_router_expert": w_router_expert, "b_router_expert": b_router_expert,
        "w_exp_gate": w_exp_gate, "w_exp_up": w_exp_up, "w_exp_down": w_exp_down,
        "ln2_g": ln2_g, "ln2_b": ln2_b,
    }
    weights = [_layer_weights(l, P) for l in range(w_in.shape[0])]
    y_prompt, st_p = _trunk(x_prompt, None, P, weights)
    caches = (cache_mla_ckv, cache_mla_kpe, cache_sb_k, cache_sb_v,
              cache_diff_k, cache_diff_v, cache_band_k, cache_band_v)
    y_sample, st_s = _trunk(x_sample, caches, P, weights)
    return (y_prompt, y_sample, *st_p, *st_s)
```

```python
import functools
import math

import jax
import jax.numpy as jnp
from jax import lax
from jax.experimental import pallas as pl
from jax.experimental.pallas import tpu as pltpu

F32 = jnp.float32
BF16 = jnp.bfloat16

CHUNK = 64
HEAD_DIM = 64
N_HEADS = 4
GROUP_WIDTH = N_HEADS * HEAD_DIM
PAIR = 2 * HEAD_DIM
MLA_Q_LORA = 384
MLA_KV_LORA = 256
MLA_NOPE = 64
MLA_ROPE = 32
MLA_QBLK = 256
DIFF_HALF = 32
BAND_ROWS = 512
REL_CLIP = 128
N_GROUPS = 4
EXPERTS_PER_GROUP = 4
N_EXPERTS = N_GROUPS * EXPERTS_PER_GROUP
D_EXPERT = 128
ROPE_THETA = 10000.0
LN_EPS = 1e-5
NEG = -1e30
LANES = 128
SB_DEAD = -104.0
V_WIDE = N_HEADS * LANES
KPE_COL = MLA_Q_LORA + MLA_KV_LORA + 9 * GROUP_WIDTH
D_IN_PAD = KPE_COL + LANES
MLA_ONE_LANE = PAIR + MLA_ROPE
EXP_GUARD = 60.0
LOG2E = 1.4426950408889634
VMEM_LIMIT = 56 * 1024 * 1024

_NT = (((1,), (1,)), ((), ()))


def _params(*sem):
    return pltpu.CompilerParams(dimension_semantics=sem, vmem_limit_bytes=VMEM_LIMIT)


def _dot(a, b):
    return jnp.dot(a, b, preferred_element_type=F32)


def _dot_nt(a, b):
    return lax.dot_general(a, b, _NT, preferred_element_type=F32)


def _layer_norm(x, g, b):
    mu = jnp.mean(x, axis=-1, keepdims=True)
    var = jnp.mean(jnp.square(x - mu), axis=-1, keepdims=True)
    return (x - mu) * lax.rsqrt(var + LN_EPS) * g + b


def _rms_norm(x, w, eps):
    return x * lax.rsqrt(jnp.mean(jnp.square(x), axis=-1, keepdims=True) + eps) * w


def _rope(x, cos, sin_signed):
    n = x.shape[-1]
    lane = lax.broadcasted_iota(jnp.int32, x.shape, 1)
    partner = jnp.where((lane & 16) == 0, pltpu.roll(x, n - 16, 1), pltpu.roll(x, 16, 1))
    return x * cos + partner * sin_signed


def _ones_row(shape):
    row = lax.broadcasted_iota(jnp.int32, shape, 0)
    return jnp.where((row & (LANES - 1)) == HEAD_DIM, 1.0, 0.0)


def _head_lanes(shape, lo, width):
    lane = lax.broadcasted_iota(jnp.int32, shape, 1)
    return (lane >= lo) & (lane < lo + width)


def _row_spec(tm, width):
    return pl.BlockSpec((tm, width), lambda i: (i, 0))


def _const_spec(shape):
    return pl.BlockSpec(shape, lambda *_: (0,) * len(shape))


def _resident_spec(rows, width):
    return pl.BlockSpec((1, rows, width), lambda b, i: (b, 0, 0), pipeline_mode=pl.Buffered(1))


def _ln_kernel(x_ref, g_ref, b_ref, o_ref):
    o_ref[...] = _layer_norm(x_ref[...], g_ref[...], b_ref[...])


def _ln_call(x, g, b, tm):
    n, d = x.shape
    return pl.pallas_call(
        _ln_kernel, grid=(n // tm,),
        in_specs=[_row_spec(tm, d), _const_spec((1, d)), _const_spec((1, d))],
        out_specs=_row_spec(tm, d),
        out_shape=jax.ShapeDtypeStruct((n, d), F32),
        compiler_params=_params("parallel"), name="ln_in",
    )(x, g.reshape(1, d), b.reshape(1, d))


_PRE_F32 = ("ckv", "kpe", "sbk", "sbv", "dk", "dv", "bk", "bv")
_PRE_BF16 = ("sbq", "sbk", "sbv", "dq", "dk", "dv", "bq", "bk", "bv")


def _pre_kernel(x_ref, cos_ref, sin_ref, win_ref, qn_ref, wuq_ref, kvn_ref, wdvt_ref,
                ckv_o, kpe_o, sbk_o, sbv_o, dk_o, dv_o, bk_o, bv_o,
                qm_o, sbq_h, sbk_h, sbv_h, dq_h, dk_h, dv_h, bq_h, bk_h, bv_h):
    xb = x_ref[...].astype(BF16)
    cos = cos_ref[...]
    sin = sin_ref[...]
    cos_l, sin_l = cos[:, :LANES], sin[:, :LANES]

    def proj(seg, width=GROUP_WIDTH):
        return _dot(xb, win_ref[:, seg:seg + width])

    mla_scale = (MLA_NOPE + MLA_ROPE) ** -0.5 * LOG2E
    cq =_rms_norm(proj(0, MLA_Q_LORA), qn_ref[...], 1e-6)
    q = _dot(cq.astype(BF16), wuq_ref[...])
    for h in range(N_HEADS):
        lo = h * MLA_QBLK
        qm_o[:, lo:lo + LANES] = (q[:, lo:lo + LANES] * mla_scale).astype(BF16)
        qr = _rope(q[:, lo + LANES:lo + MLA_QBLK], cos_l, sin_l)
        qm_o[:, lo + LANES:lo + MLA_QBLK] = (qr * mla_scale).astype(BF16)
    seg = MLA_Q_LORA
    ckv_o[...] = _rms_norm(proj(seg), kvn_ref[...], 1e-6)
    seg += MLA_KV_LORA
    kpe_o[...] = _rope(proj(KPE_COL, LANES), cos_l, sin_l)

    sbq_h[...] = (proj(seg) * HEAD_DIM ** -0.5).astype(BF16)
    k = proj(seg + GROUP_WIDTH)
    sbk_o[...] = k
    sbk_h[...] = k.astype(BF16)
    v = proj(seg + 2 * GROUP_WIDTH)
    sbv_o[...] = v
    sbv_h[...] = v.astype(BF16)
    seg += 3 * GROUP_WIDTH
    dq_h[...] = (_rope(proj(seg), cos, sin) * (DIFF_HALF ** -0.5 * LOG2E)).astype(BF16)
    k = _rope(proj(seg + GROUP_WIDTH), cos, sin)
    dk_o[...] = k
    dk_h[...] = k.astype(BF16)
    dv_o[...] = proj(seg + 2 * GROUP_WIDTH)
    dv_h[0] = (_dot_nt(wdvt_ref[...], xb) + _ones_row(dv_h.shape[1:])).astype(BF16)
    seg += 3 * GROUP_WIDTH
    bq_h[...] = (proj(seg) * (HEAD_DIM ** -0.5 * LOG2E)).astype(BF16)
    k = proj(seg + GROUP_WIDTH)
    bk_o[...] = k
    bk_h[...] = k.astype(BF16)
    v = proj(seg + 2 * GROUP_WIDTH)
    bv_o[...] = v
    bv_h[...] = v.astype(BF16)


def _pre_call(x, cos, sin, w, t, tm):
    n, d = x.shape
    nt = t // tm
    tab = pl.BlockSpec((tm, GROUP_WIDTH), lambda i: (i % nt, 0))
    widths_f32 = [GROUP_WIDTH, LANES] + [GROUP_WIDTH] * 6
    widths_h = [N_HEADS * MLA_QBLK] + [GROUP_WIDTH] * 9
    out_shape = ([jax.ShapeDtypeStruct((n, wd), F32) for wd in widths_f32]
                 + [jax.ShapeDtypeStruct((n, wd), BF16) for wd in widths_h])
    out_specs = [_row_spec(tm, wd) for wd in widths_f32 + widths_h]
    dv = len(widths_f32) + 1 + _PRE_BF16.index("dv")
    out_shape[dv] = jax.ShapeDtypeStruct((n // t, V_WIDE, t), BF16)
    out_specs[dv] = pl.BlockSpec((1, V_WIDE, tm), lambda i: (i // nt, 0, i % nt))
    outs = pl.pallas_call(
        _pre_kernel, grid=(n // tm,),
        in_specs=[_row_spec(tm, d), tab, tab,
                  _const_spec((d, D_IN_PAD)), _const_spec((1, MLA_Q_LORA)),
                  _const_spec((MLA_Q_LORA, N_HEADS * MLA_QBLK)), _const_spec((1, MLA_KV_LORA)),
                  _const_spec((V_WIDE, d))],
        out_specs=out_specs,
        out_shape=out_shape,
        compiler_params=_params("parallel"), name="pre",
    )(x, cos, sin, w["w_in"], w["q_norm"], w["w_uq"], w["kv_norm"], w["w_dv_t"])
    f32 = dict(zip(_PRE_F32, outs[:8]))
    h16 = dict(zip(("qm",) + _PRE_BF16, outs[8:]))
    return f32, h16


def _kvup_kernel(ckv_ref, kpe_ref, wuk_ref, wuv_ref, km_o, vm_o):
    cb = ckv_ref[...].astype(BF16)
    kn = _dot(cb, wuk_ref[...]).astype(BF16)
    lane = lax.broadcasted_iota(jnp.int32, kpe_ref.shape, 1)
    kpe = jnp.where(lane == MLA_ROPE, 1.0, kpe_ref[...]).astype(BF16)
    for p in range(2):
        km_o[:, p * MLA_QBLK:p * MLA_QBLK + PAIR] = kn[:, p * PAIR:(p + 1) * PAIR]
        km_o[:, p * MLA_QBLK + PAIR:(p + 1) * MLA_QBLK] = kpe
    vm_o[0] = (_dot_nt(wuv_ref[...], cb) + _ones_row(vm_o.shape[1:])).astype(BF16)


def _kvup_call(ckv, kpe, w, tm, rows):
    n = ckv.shape[0]
    nt = rows // tm
    return pl.pallas_call(
        _kvup_kernel, grid=(n // tm,),
        in_specs=[_row_spec(tm, MLA_KV_LORA), _row_spec(tm, LANES),
                  _const_spec((MLA_KV_LORA, GROUP_WIDTH)), _const_spec((V_WIDE, MLA_KV_LORA))],
        out_specs=[_row_spec(tm, 2 * MLA_QBLK), pl.BlockSpec((1, V_WIDE, tm), lambda i: (i // nt, 0, i % nt))],
        out_shape=[jax.ShapeDtypeStruct((n, 2 * MLA_QBLK), BF16),
                   jax.ShapeDtypeStruct((n // rows, V_WIDE, rows), BF16)],
        compiler_params=_params("parallel"), name="kvup",
    )(ckv, kpe, w["w_uk"], w["w_uv"])


def _fold_max(d):
    m = d[:, :LANES]
    for c in range(1, d.shape[1] // LANES):
        m = jnp.maximum(m, d[:, c * LANES:(c + 1) * LANES])
    return jnp.max(m.reshape(m.shape[0] // 8, 8, m.shape[1]), axis=0)


def _softmax_chains(chains, operands, i, jd, tq, tk, off, qs_ref, acc_ref):
    sub = max(tq, LANES)
    qstart = off + i * tq
    own = pl.multiple_of(qstart // sub * sub, sub)
    own_mask = ((own + lax.broadcasted_iota(jnp.int32, (tq, sub), 1)) // CHUNK
                <= (qstart + lax.broadcasted_iota(jnp.int32, (tq, sub), 0)) // CHUNK)
    own_mask_t = ((own + lax.broadcasted_iota(jnp.int32, (sub, tq), 0)) // CHUNK
                  <= (qstart + lax.broadcasted_iota(jnp.int32, (sub, tq), 1)) // CHUNK)

    def sweep(step, carry):
        carry = lax.fori_loop(0, jd, lambda j, cy: step(pl.multiple_of(j * tk, tk), tk, False, cy), carry)
        carry = lax.fori_loop(0, (own - jd * tk) // sub,
                              lambda s, cy: step(pl.multiple_of(jd * tk + s * sub, sub), sub, False, cy), carry)
        return step(own, sub, True, carry)

    for n in range(chains):
        q, k, _ = operands(n, own, sub)
        m0 = jnp.max(jnp.where(own_mask, _dot_nt(q, k), NEG), axis=-1, keepdims=True)
        lane = lax.broadcasted_iota(jnp.int32, q.shape, 1)
        qs_ref[n] = jnp.where(lane == operands(n, None), (-m0).astype(BF16), q)
    acc_ref[...] = jnp.zeros(acc_ref.shape, F32)

    def fast(start, rows, masked, top):
        d_next = _dot_nt(operands(0, start, rows)[1], qs_ref[0])
        for n in range(chains):
            d = d_next
            if n + 1 < chains:
                d_next = _dot_nt(operands(n + 1, start, rows)[1], qs_ref[n + 1])
            if masked:
                d = jnp.where(own_mask_t, d, NEG)
            top = jnp.maximum(top, _fold_max(d))
            acc_ref[n] += _dot(operands(n, start, rows)[2], jnp.exp2(d).astype(BF16))
        return top

    top = sweep(fast, jnp.full((8, min(tq, LANES)), NEG, F32))

    @pl.when(jnp.max(top) > EXP_GUARD)
    def _():
        def safe(start, rows, masked, states):
            new = []
            for n in range(chains):
                q, k, vt = operands(n, start, rows)
                s = _dot_nt(k, q)
                if masked:
                    s = jnp.where(own_mask_t, s, NEG)
                m, acc = states[n]
                m_new = jnp.maximum(m, jnp.max(s, axis=0, keepdims=True))
                new.append((m_new, jnp.exp2(m - m_new) * acc + _dot(vt, jnp.exp2(s - m_new).astype(BF16))))
            return tuple(new)

        states = sweep(safe, tuple((jnp.full((1, tq), NEG, F32), jnp.zeros((LANES, tq), F32))
                                   for _ in range(chains)))
        for n in range(chains):
            acc_ref[n] = states[n][1]


def _flash_scratch(chains, tq):
    return [pltpu.VMEM((chains, tq, MLA_QBLK), BF16), pltpu.VMEM((chains, LANES, tq), F32)]


def _mla_kernel(q_ref, k_ref, v_ref, o_ref, qs_ref, acc_ref, *, tq, tk, off):
    i = pl.program_id(1)
    jd = (off + i * tq) // tk

    def operands(h, start, rows=tk):
        if start is None:
            return MLA_ONE_LANE
        p = h // 2
        return (q_ref[0, :, h * MLA_QBLK:(h + 1) * MLA_QBLK],
                k_ref[0, pl.ds(start, rows), p * MLA_QBLK:(p + 1) * MLA_QBLK],
                v_ref[0, h * LANES:(h + 1) * LANES, pl.ds(start, rows)])

    _softmax_chains(N_HEADS, operands, i, jd, tq, tk, off, qs_ref, acc_ref)
    for p in range(2):
        pair = []
        for h in (2 * p, 2 * p + 1):
            acc = acc_ref[h]
            pair.append(acc[:HEAD_DIM] / acc[HEAD_DIM:HEAD_DIM + 1])
        o_ref[0, :, p * PAIR:(p + 1) * PAIR] = jnp.concatenate(pair, axis=0).T.astype(BF16)


def _diff_kernel(q_ref, k_ref, v_ref, lam_ref, subln_ref, o_ref, qs_ref, acc_ref, *, tq, tk, off, lam_init):
    i = pl.program_id(1)
    jd = (off + i * tq) // tk

    def operands(n, start, rows=tk):
        if start is None:
            return PAIR
        h, half = n // 2, n % 2
        p, hh = h // 2, h % 2
        qp = q_ref[0, :, p * PAIR:(p + 1) * PAIR]
        sel = _head_lanes((tq, PAIR), hh * HEAD_DIM + half * DIFF_HALF, DIFF_HALF)
        zero = jnp.zeros_like(qp)
        one = jnp.where(lax.broadcasted_iota(jnp.int32, (rows, PAIR), 1) == 0, 1.0, 0.0).astype(BF16)
        return (jnp.concatenate([jnp.where(sel, qp, zero), zero], axis=-1),
                jnp.concatenate([k_ref[0, pl.ds(start, rows), p * PAIR:(p + 1) * PAIR], one], axis=-1),
                v_ref[0, h * LANES:(h + 1) * LANES, pl.ds(start, rows)])

    _softmax_chains(2 * N_HEADS, operands, i, jd, tq, tk, off, qs_ref, acc_ref)

    lp = lam_ref[...]
    lam = (jnp.exp(jnp.sum(lp[0:1] * lp[1:2], axis=-1, keepdims=True))
           - jnp.exp(jnp.sum(lp[2:3] * lp[3:4], axis=-1, keepdims=True)) + lam_init)
    gain = subln_ref[...]
    for p in range(2):
        pair = []
        for h in (2 * p, 2 * p + 1):
            a1, a2 = acc_ref[2 * h], acc_ref[2 * h + 1]
            o = (a1[:HEAD_DIM] / a1[HEAD_DIM:HEAD_DIM + 1]
                 - lam * (a2[:HEAD_DIM] / a2[HEAD_DIM:HEAD_DIM + 1]))
            ms = jnp.mean(o * o, axis=0, keepdims=True)
            pair.append(o * lax.rsqrt(ms + 1e-5) * gain * (1.0 - lam_init))
        o_ref[0, :, p * PAIR:(p + 1) * PAIR] = jnp.concatenate(pair, axis=0).T.astype(BF16)


def _sb_scratch(tq):
    return [pltpu.VMEM((N_HEADS, tq, PAIR), BF16), pltpu.VMEM((N_HEADS, tq, 1), F32),
            pltpu.VMEM((N_HEADS, tq, PAIR), F32)]


def _sb_kernel(q_ref, k_ref, v_ref, o_ref, qs_ref, later_ref, acc_ref, *, tq, tk, off):
    i = pl.program_id(1)
    jd = (off + i * tq) // tk
    row = lax.broadcasted_iota(jnp.int32, (tk, tk), 0)
    col = lax.broadcasted_iota(jnp.int32, (tk, tk), 1)
    after = jnp.where(row > col, 1.0, 0.0).astype(BF16)
    qpos = off + i * tq + lax.broadcasted_iota(jnp.int32, (tq, tk), 0)
    kcol = lax.broadcasted_iota(jnp.int32, (tq, tk), 1)
    first = _head_lanes((tq, PAIR), 0, HEAD_DIM)

    for h in range(N_HEADS):
        qp = q_ref[0, :, (h // 2) * PAIR:(h // 2 + 1) * PAIR]
        qs_ref[h] = jnp.where(_head_lanes((tq, PAIR), (h % 2) * HEAD_DIM, HEAD_DIM), qp, jnp.zeros_like(qp))
    later_ref[...] = jnp.zeros(later_ref.shape, F32)
    acc_ref[...] = jnp.zeros(acc_ref.shape, F32)

    def block(j, diagonal):
        start = pl.multiple_of(j * tk, tk)
        for h in range(N_HEADS):
            p = h // 2
            kp = k_ref[0, pl.ds(start, tk), p * PAIR:(p + 1) * PAIR]
            vp = v_ref[0, pl.ds(start, tk), p * PAIR:(p + 1) * PAIR]
            z = _dot_nt(qs_ref[h], kp)
            t = jnp.log(1.0 + jnp.exp(-jnp.abs(z)))
            log_keep = -(jnp.maximum(z, 0.0) + t)
            log_beta = jnp.minimum(z, 0.0) - t
            if diagonal:
                earlier = (start + kcol) < qpos
                log_keep = jnp.where(earlier, log_keep, 0.0)
            hi = log_keep.astype(BF16)
            lo = (log_keep - hi.astype(F32)).astype(BF16)
            within = _dot(hi, after) + _dot(lo, after)
            w = jnp.exp(log_beta + within + later_ref[h])
            if diagonal:
                w = jnp.where(earlier, w, 0.0)
            acc_ref[h] += _dot(w.astype(BF16), vp)
            later_ref[h] += jnp.sum(log_keep, axis=-1, keepdims=True)
        return (jnp.max(later_ref[...]) > SB_DEAD).astype(jnp.int32)

    def body(c):
        j, _ = c
        return j - 1, block(j, False)

    lax.while_loop(lambda c: (c[0] >= 0) & (c[1] > 0), body, (jd - 1, block(jd, True)))
    for p in range(2):
        o_ref[0, :, p * PAIR:(p + 1) * PAIR] = jnp.where(first, acc_ref[2 * p], acc_ref[2 * p + 1]).astype(BF16)


def _causal_call(kernel, q, k, v, extra, extra_specs, tq, tk, off, name, scratch):
    b, t, qw = q.shape
    tk_all = k.shape[1]
    assert off % tk == 0 and tk % tq == 0 and t % tq == 0 and tk_all % tk == 0
    assert off + t <= tk_all
    return pl.pallas_call(
        functools.partial(kernel, tq=tq, tk=tk, off=off), grid=(b, t // tq),
        in_specs=[pl.BlockSpec((1, tq, qw), lambda b_, i: (b_, i, 0)),
                  _resident_spec(tk_all, k.shape[2]), _resident_spec(v.shape[1], v.shape[2])] + extra_specs,
        out_specs=pl.BlockSpec((1, tq, GROUP_WIDTH), lambda b_, i: (b_, i, 0)),
        out_shape=jax.ShapeDtypeStruct((b, t, GROUP_WIDTH), BF16),
        scratch_shapes=scratch,
        compiler_params=_params("parallel", "arbitrary"), name=name,
    )(q, k, v, *extra)


def _band_kernel(tab_ref, q_ref, k_ref, v_ref, o_ref, bias_ref, base_ref, *, tq, win, pos0):
    b = pl.program_id(0)
    i = pl.program_id(1)
    rows = 8
    wext = base_ref.shape[2]

    @pl.when((b == 0) & (i == 0))
    def _():
        rr = lax.broadcasted_iota(jnp.int32, (rows, wext), 0)
        x = lax.broadcasted_iota(jnp.int32, (rows, wext), 1)
        rel = jnp.clip(rr - x + tq + BAND_ROWS, -REL_CLIP, REL_CLIP) + REL_CLIP

        def pick(d, vals):
            hit = rel == d
            return tuple(jnp.where(hit, tab_ref[h, d], vals[h]) for h in range(N_HEADS))

        vals = lax.fori_loop(0, 2 * REL_CLIP + 1, pick,
                             tuple(jnp.zeros((rows, wext), F32) for _ in range(N_HEADS)))
        for h in range(N_HEADS):
            base_ref[h] = vals[h] * LOG2E

        def fill(c, carry):
            r0 = pl.multiple_of(c * rows, rows)
            r = r0 + lax.broadcasted_iota(jnp.int32, (rows, win), 0)
            u = lax.broadcasted_iota(jnp.int32, (rows, win), 1)
            dchunk = r // CHUNK - u // CHUNK + BAND_ROWS // CHUNK
            ok = (dchunk >= 0) & (dchunk <= BAND_ROWS // CHUNK)
            for h in range(N_HEADS):
                shifted = pltpu.roll(base_ref[h], wext - tq + r0, 1)[:, :win]
                bias_ref[h, pl.ds(r0, rows), :] = jnp.where(ok, shifted, NEG)
            return carry

        lax.fori_loop(0, tq // rows, fill, 0)

    start = pl.multiple_of(i * tq, tq)
    u = lax.broadcasted_iota(jnp.int32, (win, PAIR), 0)
    lane = lax.broadcasted_iota(jnp.int32, (win, PAIR), 1)
    missing = jnp.where(((pos0 + i * tq - BAND_ROWS + u) < 0) & (lane == 0), NEG, 0.0).astype(BF16)
    q_one = jnp.where(lax.broadcasted_iota(jnp.int32, (tq, PAIR), 1) == 0, 1.0, 0.0).astype(BF16)
    first = _head_lanes((tq, PAIR), 0, HEAD_DIM)

    def scores(h):
        p, hh = h // 2, h % 2
        qp = q_ref[0, :, p * PAIR:(p + 1) * PAIR]
        kw = jnp.concatenate([k_ref[0, pl.ds(start, win), p * PAIR:(p + 1) * PAIR], missing], axis=-1)
        qh = jnp.where(_head_lanes((tq, PAIR), hh * HEAD_DIM, HEAD_DIM), qp, jnp.zeros_like(qp))
        return _dot_nt(jnp.concatenate([qh, q_one], axis=-1), kw)

    outs = []
    s_next = scores(0)
    for h in range(N_HEADS):
        s = s_next + bias_ref[h]
        if h + 1 < N_HEADS:
            s_next = scores(h + 1)
        e = jnp.exp2(s - jnp.max(s, axis=-1, keepdims=True))
        vw = v_ref[0, pl.ds(start, win), (h // 2) * PAIR:(h // 2 + 1) * PAIR]
        outs.append(_dot(e.astype(BF16), vw) / jnp.sum(e, axis=-1, keepdims=True))
    for p in range(2):
        o_ref[0, :, p * PAIR:(p + 1) * PAIR] = jnp.where(first, outs[2 * p], outs[2 * p + 1]).astype(BF16)


def _band_call(q, k_pad, v_pad, rel_table, tq, win, pos0):
    b, t, _ = q.shape
    rows = k_pad.shape[1]
    assert t % tq == 0 and tq % CHUNK == 0 and win % LANES == 0 and win >= tq + BAND_ROWS
    assert rows >= t - tq + win and pos0 % CHUNK == 0
    return pl.pallas_call(
        functools.partial(_band_kernel, tq=tq, win=win, pos0=pos0), grid=(b, t // tq),
        in_specs=[pl.BlockSpec(memory_space=pltpu.SMEM),
                  pl.BlockSpec((1, tq, GROUP_WIDTH), lambda b_, i: (b_, i, 0)),
                  _resident_spec(rows, GROUP_WIDTH), _resident_spec(rows, GROUP_WIDTH)],
        out_specs=pl.BlockSpec((1, tq, GROUP_WIDTH), lambda b_, i: (b_, i, 0)),
        out_shape=jax.ShapeDtypeStruct((b, t, GROUP_WIDTH), BF16),
        scratch_shapes=[pltpu.VMEM((N_HEADS, tq, win), F32),
                        pltpu.VMEM((N_HEADS, 8, -(-(win + tq) // LANES) * LANES), F32)],
        compiler_params=_params("arbitrary", "arbitrary"), name="band",
    )(rel_table, q, k_pad, v_pad)


def _post_kernel(h_ref, oa_ref, ob_ref, oc_ref, od_ref, wout_ref, g1_ref, b1_ref,
                 wr_hi_ref, wr_lo_ref, br_ref, wg_ref, wu_ref, wd_ref, g2_ref, b2_ref, o_ref, *, alpha):
    tm = h_ref.shape[0]
    y = jnp.zeros(h_ref.shape, F32)
    for g, ref in enumerate((oa_ref, ob_ref, oc_ref, od_ref)):
        y = y + _dot(ref[...], wout_ref[g * GROUP_WIDTH:(g + 1) * GROUP_WIDTH, :])
    x = _layer_norm(alpha * h_ref[...] + y, g1_ref[...], b1_ref[...])

    x_hi = x.astype(BF16)
    x_lo = (x - x_hi.astype(F32)).astype(BF16)
    logit = (_dot(x_hi, wr_hi_ref[...]) + _dot(x_lo, wr_hi_ref[...]) + _dot(x_hi, wr_lo_ref[...])
             + br_ref[...])
    lane = lax.broadcasted_iota(jnp.int32, (tm, LANES), 1).astype(F32)
    ninf = -jnp.inf

    def lane_max(a):
        return jnp.max(a, axis=-1, keepdims=True)

    def first_lane(hit):
        return jnp.min(jnp.where(hit, lane, float(LANES)), axis=-1, keepdims=True)

    gl = jnp.where(lane < N_GROUPS, logit, ninf)
    g_max = lane_max(gl)
    g_idx = first_lane(gl == g_max)
    g_weight = 1.0 / jnp.sum(jnp.exp(gl - g_max), axis=-1, keepdims=True)
    e_lo = N_GROUPS + EXPERTS_PER_GROUP * g_idx
    el = jnp.where((lane >= e_lo) & (lane < e_lo + EXPERTS_PER_GROUP), logit, ninf)
    v1 = lane_max(el)
    i1 = first_lane(el == v1)
    el2 = jnp.where(lane == i1, ninf, el)
    v2 = lane_max(el2)
    i2 = first_lane(el2 == v2)
    e2 = jnp.exp(v2 - v1)
    w1 = g_weight / (1.0 + e2)
    w2 = g_weight * e2 / (1.0 + e2)
    comb = jnp.where(lane == i1, w1, 0.0) + jnp.where(lane == i2, w2, 0.0)

    xb = x_hi
    gate = _dot(xb, wg_ref[...])
    up = _dot(xb, wu_ref[...])
    hidden = []
    for e in range(N_EXPERTS):
        c_e = jnp.sum(jnp.where(lane == float(N_GROUPS + e), comb, 0.0), axis=-1, keepdims=True)
        ge = gate[:, e * D_EXPERT:(e + 1) * D_EXPERT]
        ue = up[:, e * D_EXPERT:(e + 1) * D_EXPERT]
        hidden.append(((ge * (1.0 / (1.0 + jnp.exp(-ge)))) * ue * c_e).astype(BF16))
    f = _dot(jnp.concatenate(hidden, axis=-1), wd_ref[...])
    o_ref[...] = _layer_norm(alpha * x + f, g2_ref[...], b2_ref[...])


def _post_call(h, mix, w, tm, alpha):
    n, d = h.shape
    de = N_EXPERTS * D_EXPERT
    return pl.pallas_call(
        functools.partial(_post_kernel, alpha=alpha), grid=(n // tm,),
        in_specs=[_row_spec(tm, d)] + [_row_spec(tm, GROUP_WIDTH)] * 4
        + [_const_spec((d, d)), _const_spec((1, d)), _const_spec((1, d)),
           _const_spec((d, LANES)), _const_spec((d, LANES)), _const_spec((1, LANES)),
           _const_spec((d, de)), _const_spec((d, de)), _const_spec((de, d)),
           _const_spec((1, d)), _const_spec((1, d))],
        out_specs=_row_spec(tm, d),
        out_shape=jax.ShapeDtypeStruct((n, d), F32),
        compiler_params=_params("parallel"), name="post",
    )(h, *mix, w["w_out"], w["ln1_g"], w["ln1_b"], w["wr_hi"], w["wr_lo"], w["b_r"],
      w["w_gate"], w["w_up"], w["w_down"], w["ln2_g"], w["ln2_b"])


def _wide_columns(w):
    rows = w.shape[0]
    w = w.reshape(rows, N_HEADS, HEAD_DIM)
    return jnp.concatenate([w, jnp.zeros_like(w)], axis=-1).reshape(rows, V_WIDE)


def _wide_values_t(v):
    one = jnp.ones(v.shape[:-1] + (1,), v.dtype)
    zero = jnp.zeros(v.shape[:-1] + (LANES - HEAD_DIM - 1,), v.dtype)
    wide = jnp.concatenate([v, one, zero], axis=-1).reshape(v.shape[0], v.shape[1], V_WIDE)
    return wide.transpose(0, 2, 1)


def _layer_weights(l, P):
    d = P["w_in"].shape[1]
    w_in = P["w_in"][l]
    kpe_lo = MLA_Q_LORA + MLA_KV_LORA
    dv_lo = kpe_lo + MLA_ROPE + 5 * GROUP_WIDTH
    w_in_pad = jnp.concatenate(
        [w_in[:, :kpe_lo], w_in[:, kpe_lo + MLA_ROPE:], w_in[:, kpe_lo:kpe_lo + MLA_ROPE],
         jnp.zeros((d, LANES - MLA_ROPE), F32)], axis=1)
    w_uq = P["mla_w_uq"][l]
    blocks = []
    for h in range(N_HEADS):
        src = h * (MLA_NOPE + MLA_ROPE)
        nope = w_uq[:, src:src + MLA_NOPE]
        zero = jnp.zeros_like(nope)
        blocks += ([nope, zero] if h % 2 == 0 else [zero, nope])
        blocks += [w_uq[:, src + MLA_NOPE:src + MLA_NOPE + MLA_ROPE],
                   jnp.zeros((MLA_Q_LORA, LANES - MLA_ROPE), F32)]
    w_r = jnp.concatenate([P["w_router_group"][l], P["w_router_expert"][l],
                           jnp.zeros((d, LANES - N_GROUPS - N_EXPERTS), F32)], axis=1)
    wr_hi = w_r.astype(BF16)
    b_r = jnp.concatenate([P["b_router_group"][l], P["b_router_expert"][l],
                           jnp.zeros((LANES - N_GROUPS - N_EXPERTS,), F32)]).reshape(1, LANES)
    de = N_EXPERTS * D_EXPERT
    return {
        "w_in": w_in_pad.astype(BF16),
        "q_norm": P["mla_q_norm"][l].reshape(1, -1),
        "w_uq": jnp.concatenate(blocks, axis=1).astype(BF16),
        "kv_norm": P["mla_kv_norm"][l].reshape(1, -1),
        "w_uk": P["mla_w_uk"][l].astype(BF16),
        "w_uv": _wide_columns(P["mla_w_uv"][l]).T.astype(BF16),
        "w_dv_t": _wide_columns(w_in[:, dv_lo:dv_lo + GROUP_WIDTH]).T.astype(BF16),
        "lam": jnp.stack([P["diff_lam_q1"][l], P["diff_lam_k1"][l],
                          P["diff_lam_q2"][l], P["diff_lam_k2"][l]]),
        "subln": P["diff_subln"][l].reshape(HEAD_DIM, 1),
        "rel": P["band_rel_bias"][l],
        "w_out": P["w_out"][l].astype(BF16),
        "ln1_g": P["ln1_g"][l].reshape(1, -1), "ln1_b": P["ln1_b"][l].reshape(1, -1),
        "wr_hi": wr_hi, "wr_lo": (w_r - wr_hi.astype(F32)).astype(BF16), "b_r": b_r,
        "w_gate": P["w_exp_gate"][l].transpose(2, 0, 1, 3).reshape(d, de).astype(BF16),
        "w_up": P["w_exp_up"][l].transpose(2, 0, 1, 3).reshape(d, de).astype(BF16),
        "w_down": P["w_exp_down"][l].reshape(de, d).astype(BF16),
        "ln2_g": P["ln2_g"][l].reshape(1, -1), "ln2_b": P["ln2_b"][l].reshape(1, -1),
    }


def _rope_tables(pos):
    inv_freq = ROPE_THETA ** (-jnp.arange(0, MLA_ROPE, 2, dtype=F32) / MLA_ROPE)
    ang = pos.astype(F32)[:, None] * inv_freq[None, :]
    cos, sin = jnp.cos(ang), jnp.sin(ang)
    reps = GROUP_WIDTH // MLA_ROPE
    return (jnp.tile(jnp.concatenate([cos, cos], axis=1), (1, reps)),
            jnp.tile(jnp.concatenate([-sin, sin], axis=1), (1, reps)))


def _flat_heads(a):
    return a.reshape(a.shape[0], a.shape[1], GROUP_WIDTH)


def _pad_rows(a, rows):
    return jnp.pad(a, ((0, 0), (0, rows - a.shape[1]), (0, 0)))


def _trunk(x, caches, P, weights):
    b, t, d = x.shape
    depth = len(weights)
    alpha = (2 * depth) ** 0.25
    n = b * t
    past = 0 if caches is None else caches[0].shape[2]
    if caches is None:
        tm, tq, tk, band_tq = 256, 512, 1024, 256
        sb_tq = sb_tk = 256
        tk_all = t
    else:
        tm, tq, band_tq = t, t, t
        tk = next(c for c in (512, 256, 2 * t) if past % c == 0)
        sb_tq, sb_tk = t, 2 * t
        tk_all = -(-(past + t) // tk) * tk
        assert caches[6].shape[2] == BAND_ROWS
    assert t >= BAND_ROWS or caches is not None
    band_win = -(-(band_tq + BAND_ROWS) // LANES) * LANES
    band_rows = t - band_tq + band_win
    cos, sin = _rope_tables(past + jnp.arange(t, dtype=jnp.int32))
    h = _ln_call(x.reshape(n, d), P["ln_in_g"], P["ln_in_b"], tm)
    new_rows = [[] for _ in range(8)]
    for l in range(depth):
        w = weights[l]
        f32, h16 = _pre_call(h, cos, sin, w, t, tm)

        def new3(a):
            return a.reshape(b, t, a.shape[-1])

        if caches is None:
            def keys(name, cache_idx):
                return new3(h16[name])
            diff_v = h16["dv"]
            ckv_all, kpe_all = f32["ckv"], f32["kpe"]
            band_k = jnp.pad(new3(h16["bk"]), ((0, 0), (BAND_ROWS, band_rows - BAND_ROWS - t), (0, 0)))
            band_v = jnp.pad(new3(h16["bv"]), ((0, 0), (BAND_ROWS, band_rows - BAND_ROWS - t), (0, 0)))
        else:
            c = [a[l] for a in caches]

            def keys(name, cache_idx):
                old = _flat_heads(c[cache_idx]).astype(BF16)
                return _pad_rows(jnp.concatenate([old, new3(h16[name])], axis=1), tk_all)
            diff_v = jnp.concatenate([_wide_values_t(c[5]).astype(BF16), h16["dv"]], axis=2)
            diff_v = jnp.pad(diff_v, ((0, 0), (0, 0), (0, tk_all - past - t)))
            ckv_all =_pad_rows(jnp.concatenate([c[0], new3(f32["ckv"])], axis=1), tk_all)
            ckv_all = ckv_all.reshape(b * tk_all, MLA_KV_LORA)
            kpe_old = jnp.pad(c[1], ((0, 0), (0, 0), (0, LANES - MLA_ROPE)))
            kpe_all = _pad_rows(jnp.concatenate([kpe_old, new3(f32["kpe"])], axis=1), tk_all)
            kpe_all = kpe_all.reshape(b * tk_all, LANES)
            band_k = _pad_rows(jnp.concatenate([_flat_heads(c[6]).astype(BF16), new3(h16["bk"])], axis=1), band_rows)
            band_v = _pad_rows(jnp.concatenate([_flat_heads(c[7]).astype(BF16), new3(h16["bv"])], axis=1), band_rows)

        km, vm = _kvup_call(ckv_all, kpe_all, w, tm if caches is None else tk, tk_all)
        o_a = _causal_call(_mla_kernel, new3(h16["qm"]), km.reshape(b, tk_all, -1), vm,
                           [], [], tq, tk, past, "mla", _flash_scratch(N_HEADS, tq))
        o_b = _causal_call(_sb_kernel, new3(h16["sbq"]), keys("sbk", 2), keys("sbv", 3), [], [],
                           sb_tq, sb_tk, past, "sb", _sb_scratch(sb_tq))
        lam_init = 0.8 - 0.6 * math.exp(-0.3 * l)
        o_c = _causal_call(functools.partial(_diff_kernel, lam_init=lam_init), new3(h16["dq"]),
                           keys("dk", 4), diff_v, [w["lam"], w["subln"]],
                           [_const_spec((4, DIFF_HALF)), _const_spec((HEAD_DIM, 1))], tq, tk, past, "diff",
                           _flash_scratch(2 * N_HEADS, tq))
        o_d = _band_call(new3(h16["bq"]), band_k, band_v, w["rel"], band_tq, band_win, past)
        mix = [o.reshape(n, GROUP_WIDTH) for o in (o_a, o_b, o_c, o_d)]
        h = _post_call(h, mix, w, tm, alpha)

        keep = min(BAND_ROWS, t)
        rows = (new3(f32["ckv"]), new3(f32["kpe"])[:, :, :MLA_ROPE],
                new3(f32["sbk"]), new3(f32["sbv"]), new3(f32["dk"]), new3(f32["dv"]),
                new3(f32["bk"])[:, t - keep:], new3(f32["bv"])[:, t - keep:])
        for i, (lst, arr) in enumerate(zip(new_rows, rows)):
            lst.append(arr if i < 2 else arr.reshape(b, arr.shape[1], N_HEADS, HEAD_DIM))
    return h.reshape(b, t, d), [jnp.stack(lst) for lst in new_rows]


def kernel(x_prompt, x_sample, cache_mla_ckv, cache_mla_kpe, cache_sb_k, cache_sb_v, cache_diff_k, cache_diff_v, cache_band_k, cache_band_v, ln_in_g, ln_in_b, w_in, mla_q_norm, mla_w_uq, mla_kv_norm, mla_w_uk, mla_w_uv, diff_lam_q1, diff_lam_k1, diff_lam_q2, diff_lam_k2, diff_subln, band_rel_bias, w_out, ln1_g, ln1_b, w_router_group, b_router_group, w_router_expert, b_router_expert, w_exp_gate, w_exp_up, w_exp_down, ln2_g, ln2_b):
    P = {
        "ln_in_g": ln_in_g, "ln_in_b": ln_in_b, "w_in": w_in,
        "mla_q_norm": mla_q_norm, "mla_w_uq": mla_w_uq, "mla_kv_norm": mla_kv_norm,
        "mla_w_uk": mla_w_uk, "mla_w_uv": mla_w_uv,
        "diff_lam_q1": diff_lam_q1, "diff_lam_k1": diff_lam_k1,
        "diff_lam_q2": diff_lam_q2, "diff_lam_k2": diff_lam_k2, "diff_subln": diff_subln,
        "band_rel_bias": band_rel_bias, "w_out": w_out, "ln1_g": ln1_g, "ln1_b": ln1_b,
        "w_router_group": w_router_group, "b_router_group": b_router_group,
        "w_router_expert": w_router_expert, "b_router_expert": b_router_expert,
        "w_exp_gate": w_exp_gate, "w_exp_up": w_exp_up, "w_exp_down": w_exp_down,
        "ln2_g": ln2_g, "ln2_b": ln2_b,
    }
    weights = [_layer_weights(l, P) for l in range(w_in.shape[0])]
    y_prompt, st_p = _trunk(x_prompt, None, P, weights)
    caches = (cache_mla_ckv, cache_mla_kpe, cache_sb_k, cache_sb_v,
              cache_diff_k, cache_diff_v, cache_band_k, cache_band_v)
    y_sample, st_s = _trunk(x_sample, caches, P, weights)
    return (y_prompt, y_sample, *st_p, *st_s)
```

```python
import functools
import math

import jax
import jax.numpy as jnp
from jax import lax
from jax.experimental import pallas as pl
from jax.experimental.pallas import tpu as pltpu

F32 = jnp.float32
BF16 = jnp.bfloat16

CHUNK = 64
HEAD_DIM = 64
N_HEADS = 4
GROUP_WIDTH = N_HEADS * HEAD_DIM
PAIR = 2 * HEAD_DIM
MLA_Q_LORA = 384
MLA_KV_LORA = 256
MLA_NOPE = 64
MLA_ROPE = 32
MLA_QBLK = 256
DIFF_HALF = 32
BAND_ROWS = 512
REL_CLIP = 128
N_GROUPS = 4
EXPERTS_PER_GROUP = 4
N_EXPERTS = N_GROUPS * EXPERTS_PER_GROUP
D_EXPERT = 128
ROPE_THETA = 10000.0
LN_EPS = 1e-5
NEG = -1e30
LANES = 128
SB_DEAD = -104.0
V_WIDE = N_HEADS * LANES
KPE_COL = MLA_Q_LORA + MLA_KV_LORA + 9 * GROUP_WIDTH
D_IN_PAD = KPE_COL + LANES
MLA_ONE_LANE = PAIR + MLA_ROPE
EXP_GUARD = 60.0
LOG2E = 1.4426950408889634
VMEM_LIMIT = 56 * 1024 * 1024

_NT = (((1,), (1,)), ((), ()))


def _params(*sem):
    return pltpu.CompilerParams(dimension_semantics=sem, vmem_limit_bytes=VMEM_LIMIT)


def _dot(a, b):
    return jnp.dot(a, b, preferred_element_type=F32)


def _dot_nt(a, b):
    return lax.dot_general(a, b, _NT, preferred_element_type=F32)


def _layer_norm(x, g, b):
    mu = jnp.mean(x, axis=-1, keepdims=True)
    var = jnp.mean(jnp.square(x - mu), axis=-1, keepdims=True)
    return (x - mu) * lax.rsqrt(var + LN_EPS) * g + b


def _rms_norm(x, w, eps):
    return x * lax.rsqrt(jnp.mean(jnp.square(x), axis=-1, keepdims=True) + eps) * w


def _rope(x, cos, sin_signed):
    n = x.shape[-1]
    lane = lax.broadcasted_iota(jnp.int32, x.shape, 1)
    partner = jnp.where((lane & 16) == 0, pltpu.roll(x, n - 16, 1), pltpu.roll(x, 16, 1))
    return x * cos + partner * sin_signed


def _ones_row(shape):
    row = lax.broadcasted_iota(jnp.int32, shape, 0)
    return jnp.where((row & (LANES - 1)) == HEAD_DIM, 1.0, 0.0)


def _head_lanes(shape, lo, width):
    lane = lax.broadcasted_iota(jnp.int32, shape, 1)
    return (lane >= lo) & (lane < lo + width)


def _row_spec(tm, width):
    return pl.BlockSpec((tm, width), lambda i: (i, 0))


def _const_spec(shape):
    return pl.BlockSpec(shape, lambda *_: (0,) * len(shape))


def _resident_spec(rows, width):
    return pl.BlockSpec((1, rows, width), lambda b, i: (b, 0, 0), pipeline_mode=pl.Buffered(1))


def _ln_kernel(x_ref, g_ref, b_ref, o_ref):
    o_ref[...] = _layer_norm(x_ref[...], g_ref[...], b_ref[...])


def _ln_call(x, g, b, tm):
    n, d = x.shape
    return pl.pallas_call(
        _ln_kernel, grid=(n // tm,),
        in_specs=[_row_spec(tm, d), _const_spec((1, d)), _const_spec((1, d))],
        out_specs=_row_spec(tm, d),
        out_shape=jax.ShapeDtypeStruct((n, d), F32),
        compiler_params=_params("parallel"), name="ln_in",
    )(x, g.reshape(1, d), b.reshape(1, d))


_PRE_F32 = ("ckv", "kpe", "sbk", "sbv", "dk", "dv", "bk", "bv")
_PRE_BF16 = ("sbq", "sbk", "sbv", "dq", "dk", "dv", "bq", "bk", "bv")


def _pre_kernel(x_ref, cos_ref, sin_ref, win_ref, qn_ref, wuq_ref, kvn_ref, wdvt_ref, bk_zero, bv_zero,
                ckv_o, kpe_o, sbk_o, sbv_o, dk_o, dv_o, bk_o, bv_o,
                qm_o, sbq_h, sbk_h, sbv_h, dq_h, dk_h, dv_h, bq_h, bk_h, bv_h):
    xb = x_ref[...].astype(BF16)
    cos = cos_ref[...]
    sin = sin_ref[...]
    cos_l, sin_l = cos[:, :LANES], sin[:, :LANES]

    def proj(seg, width=GROUP_WIDTH):
        return _dot(xb, win_ref[:, seg:seg + width])

    mla_scale = (MLA_NOPE + MLA_ROPE) ** -0.5 * LOG2E
    cq =_rms_norm(proj(0, MLA_Q_LORA), qn_ref[...], 1e-6)
    q = _dot(cq.astype(BF16), wuq_ref[...])
    for h in range(N_HEADS):
        lo = h * MLA_QBLK
        qm_o[:, lo:lo + LANES] = (q[:, lo:lo + LANES] * mla_scale).astype(BF16)
        qr = _rope(q[:, lo + LANES:lo + MLA_QBLK], cos_l, sin_l)
        qm_o[:, lo + LANES:lo + MLA_QBLK] = (qr * mla_scale).astype(BF16)
    seg = MLA_Q_LORA
    ckv_o[...] = _rms_norm(proj(seg), kvn_ref[...], 1e-6)
    seg += MLA_KV_LORA
    kpe_o[...] = _rope(proj(KPE_COL, LANES), cos_l, sin_l)

    sbq_h[...] = (proj(seg) * HEAD_DIM ** -0.5).astype(BF16)
    k = proj(seg + GROUP_WIDTH)
    sbk_o[...] = k
    sbk_h[...] = k.astype(BF16)
    v = proj(seg + 2 * GROUP_WIDTH)
    sbv_o[...] = v
    sbv_h[...] = v.astype(BF16)
    seg += 3 * GROUP_WIDTH
    dq_h[...] = (_rope(proj(seg), cos, sin) * (DIFF_HALF ** -0.5 * LOG2E)).astype(BF16)
    k = _rope(proj(seg + GROUP_WIDTH), cos, sin)
    dk_o[...] = k
    dk_h[...] = k.astype(BF16)
    dv_o[...] = proj(seg + 2 * GROUP_WIDTH)
    dv_h[0] = (_dot_nt(wdvt_ref[...], xb) + _ones_row(dv_h.shape[1:])).astype(BF16)
    seg += 3 * GROUP_WIDTH
    bq_h[...] = (proj(seg) * (HEAD_DIM ** -0.5 * LOG2E)).astype(BF16)
    k = proj(seg + GROUP_WIDTH)
    bk_o[...] = k
    bk_h[0] = k.astype(BF16)
    v = proj(seg + 2 * GROUP_WIDTH)
    bv_o[...] = v
    bv_h[0] = v.astype(BF16)


def _pre_call(x, cos, sin, w, t, tm, front):
    n, d = x.shape
    nt = t // tm
    assert front % tm == 0
    tab = pl.BlockSpec((tm, GROUP_WIDTH), lambda i: (i % nt, 0))
    widths_f32 = [GROUP_WIDTH, LANES] + [GROUP_WIDTH] * 6
    widths_h = [N_HEADS * MLA_QBLK] + [GROUP_WIDTH] * 9
    out_shape = ([jax.ShapeDtypeStruct((n, wd), F32) for wd in widths_f32]
                 + [jax.ShapeDtypeStruct((n, wd), BF16) for wd in widths_h])
    out_specs = [_row_spec(tm, wd) for wd in widths_f32 + widths_h]
    dv = len(widths_f32) + 1 + _PRE_BF16.index("dv")
    out_shape[dv] = jax.ShapeDtypeStruct((n // t, V_WIDE, t), BF16)
    out_specs[dv] = pl.BlockSpec((1, V_WIDE, tm), lambda i: (i // nt, 0, i % nt))
    band = [len(widths_f32) + 1 + _PRE_BF16.index(name) for name in ("bk", "bv")]
    padded = jax.ShapeDtypeStruct((n // t, front + t, GROUP_WIDTH), BF16)
    for o in band:
        out_shape[o] = padded
        out_specs[o] = pl.BlockSpec((1, tm, GROUP_WIDTH), lambda i: (i // nt, front // tm + i % nt, 0))
    inputs = (x, cos, sin, w["w_in"], w["q_norm"], w["w_uq"], w["kv_norm"], w["w_dv_t"])
    zeros = jnp.zeros(padded.shape, BF16)
    outs = pl.pallas_call(
        _pre_kernel, grid=(n // tm,),
        in_specs=[_row_spec(tm, d), tab, tab,
                  _const_spec((d, D_IN_PAD)), _const_spec((1, MLA_Q_LORA)),
                  _const_spec((MLA_Q_LORA, N_HEADS * MLA_QBLK)), _const_spec((1, MLA_KV_LORA)),
                  _const_spec((V_WIDE, d)), pl.BlockSpec(memory_space=pl.ANY), pl.BlockSpec(memory_space=pl.ANY)],
        out_specs=out_specs,
        out_shape=out_shape,
        input_output_aliases={len(inputs): band[0], len(inputs) + 1: band[1]},
        compiler_params=_params("parallel"), name="pre",
    )(*inputs, zeros, zeros)
    f32 = dict(zip(_PRE_F32, outs[:8]))
    h16 = dict(zip(("qm",) + _PRE_BF16, outs[8:]))
    return f32, h16


def _kvup_kernel(ckv_ref, kpe_ref, wuk_ref, wuv_ref, km_o, vm_o):
    cb = ckv_ref[...].astype(BF16)
    kn = _dot(cb, wuk_ref[...]).astype(BF16)
    lane = lax.broadcasted_iota(jnp.int32, kpe_ref.shape, 1)
    kpe = jnp.where(lane == MLA_ROPE, 1.0, kpe_ref[...]).astype(BF16)
    for p in range(2):
        km_o[:, p * MLA_QBLK:p * MLA_QBLK + PAIR] = kn[:, p * PAIR:(p + 1) * PAIR]
        km_o[:, p * MLA_QBLK + PAIR:(p + 1) * MLA_QBLK] = kpe
    vm_o[0] = (_dot_nt(wuv_ref[...], cb) + _ones_row(vm_o.shape[1:])).astype(BF16)


def _kvup_call(ckv, kpe, w, tm, rows):
    n = ckv.shape[0]
    nt = rows // tm
    return pl.pallas_call(
        _kvup_kernel, grid=(n // tm,),
        in_specs=[_row_spec(tm, MLA_KV_LORA), _row_spec(tm, LANES),
                  _const_spec((MLA_KV_LORA, GROUP_WIDTH)), _const_spec((V_WIDE, MLA_KV_LORA))],
        out_specs=[_row_spec(tm, 2 * MLA_QBLK), pl.BlockSpec((1, V_WIDE, tm), lambda i: (i // nt, 0, i % nt))],
        out_shape=[jax.ShapeDtypeStruct((n, 2 * MLA_QBLK), BF16),
                   jax.ShapeDtypeStruct((n // rows, V_WIDE, rows), BF16)],
        compiler_params=_params("parallel"), name="kvup",
    )(ckv, kpe, w["w_uk"], w["w_uv"])


def _fold_max(d):
    m = d[:, :LANES]
    for c in range(1, d.shape[1] // LANES):
        m = jnp.maximum(m, d[:, c * LANES:(c + 1) * LANES])
    return jnp.max(m.reshape(m.shape[0] // 8, 8, m.shape[1]), axis=0)


def _softmax_chains(chains, operands, i, jd, tq, tk, off, qs_ref, acc_ref):
    sub = max(tq, LANES)
    qstart = off + i * tq
    own = pl.multiple_of(qstart // sub * sub, sub)
    own_mask = ((own + lax.broadcasted_iota(jnp.int32, (tq, sub), 1)) // CHUNK
                <= (qstart + lax.broadcasted_iota(jnp.int32, (tq, sub), 0)) // CHUNK)
    own_mask_t = ((own + lax.broadcasted_iota(jnp.int32, (sub, tq), 0)) // CHUNK
                  <= (qstart + lax.broadcasted_iota(jnp.int32, (sub, tq), 1)) // CHUNK)

    def sweep(step, carry):
        carry = lax.fori_loop(0, jd, lambda j, cy: step(pl.multiple_of(j * tk, tk), tk, False, cy), carry)
        carry = lax.fori_loop(0, (own - jd * tk) // sub,
                              lambda s, cy: step(pl.multiple_of(jd * tk + s * sub, sub), sub, False, cy), carry)
        return step(own, sub, True, carry)

    for n in range(chains):
        q, k, _ = operands(n, own, sub)
        m0 = jnp.max(jnp.where(own_mask, _dot_nt(q, k), NEG), axis=-1, keepdims=True)
        lane = lax.broadcasted_iota(jnp.int32, q.shape, 1)
        qs_ref[n] = jnp.where(lane == operands(n, None), (-m0).astype(BF16), q)
    acc_ref[...] = jnp.zeros(acc_ref.shape, F32)

    def fast(start, rows, masked, top):
        d_next = _dot_nt(operands(0, start, rows)[1], qs_ref[0])
        for n in range(chains):
            d = d_next
            if n + 1 < chains:
                d_next = _dot_nt(operands(n + 1, start, rows)[1], qs_ref[n + 1])
            if masked:
                d = jnp.where(own_mask_t, d, NEG)
            top = jnp.maximum(top, _fold_max(d))
            acc_ref[n] += _dot(operands(n, start, rows)[2], jnp.exp2(d).astype(BF16))
        return top

    top = sweep(fast, jnp.full((8, min(tq, LANES)), NEG, F32))

    @pl.when(jnp.max(top) > EXP_GUARD)
    def _():
        def safe(start, rows, masked, states):
            new = []
            for n in range(chains):
                q, k, vt = operands(n, start, rows)
                s = _dot_nt(k, q)
                if masked:
                    s = jnp.where(own_mask_t, s, NEG)
                m, acc = states[n]
                m_new = jnp.maximum(m, jnp.max(s, axis=0, keepdims=True))
                new.append((m_new, jnp.exp2(m - m_new) * acc + _dot(vt, jnp.exp2(s - m_new).astype(BF16))))
            return tuple(new)

        states = sweep(safe, tuple((jnp.full((1, tq), NEG, F32), jnp.zeros((LANES, tq), F32))
                                   for _ in range(chains)))
        for n in range(chains):
            acc_ref[n] = states[n][1]


def _flash_scratch(chains, tq):
    return [pltpu.VMEM((chains, tq, MLA_QBLK), BF16), pltpu.VMEM((chains, LANES, tq), F32)]


def _mla_kernel(q_ref, k_ref, v_ref, o_ref, qs_ref, acc_ref, *, tq, tk, off):
    i = pl.program_id(1)
    jd = (off + i * tq) // tk

    def operands(h, start, rows=tk):
        if start is None:
            return MLA_ONE_LANE
        p = h // 2
        return (q_ref[0, :, h * MLA_QBLK:(h + 1) * MLA_QBLK],
                k_ref[0, pl.ds(start, rows), p * MLA_QBLK:(p + 1) * MLA_QBLK],
                v_ref[0, h * LANES:(h + 1) * LANES, pl.ds(start, rows)])

    _softmax_chains(N_HEADS, operands, i, jd, tq, tk, off, qs_ref, acc_ref)
    for p in range(2):
        pair = []
        for h in (2 * p, 2 * p + 1):
            acc = acc_ref[h]
            pair.append(acc[:HEAD_DIM] / acc[HEAD_DIM:HEAD_DIM + 1])
        o_ref[0, :, p * PAIR:(p + 1) * PAIR] = jnp.concatenate(pair, axis=0).T.astype(BF16)


def _diff_kernel(q_ref, k_ref, v_ref, lam_ref, subln_ref, o_ref, qs_ref, acc_ref, *, tq, tk, off, lam_init):
    i = pl.program_id(1)
    jd = (off + i * tq) // tk

    def operands(n, start, rows=tk):
        if start is None:
            return PAIR
        h, half = n // 2, n % 2
        p, hh = h // 2, h % 2
        qp = q_ref[0, :, p * PAIR:(p + 1) * PAIR]
        sel = _head_lanes((tq, PAIR), hh * HEAD_DIM + half * DIFF_HALF, DIFF_HALF)
        zero = jnp.zeros_like(qp)
        one = jnp.where(lax.broadcasted_iota(jnp.int32, (rows, PAIR), 1) == 0, 1.0, 0.0).astype(BF16)
        return (jnp.concatenate([jnp.where(sel, qp, zero), zero], axis=-1),
                jnp.concatenate([k_ref[0, pl.ds(start, rows), p * PAIR:(p + 1) * PAIR], one], axis=-1),
                v_ref[0, h * LANES:(h + 1) * LANES, pl.ds(start, rows)])

    _softmax_chains(2 * N_HEADS, operands, i, jd, tq, tk, off, qs_ref, acc_ref)

    lp = lam_ref[...]
    lam = (jnp.exp(jnp.sum(lp[0:1] * lp[1:2], axis=-1, keepdims=True))
           - jnp.exp(jnp.sum(lp[2:3] * lp[3:4], axis=-1, keepdims=True)) + lam_init)
    gain = subln_ref[...]
    for p in range(2):
        pair = []
        for h in (2 * p, 2 * p + 1):
            a1, a2 = acc_ref[2 * h], acc_ref[2 * h + 1]
            o = (a1[:HEAD_DIM] / a1[HEAD_DIM:HEAD_DIM + 1]
                 - lam * (a2[:HEAD_DIM] / a2[HEAD_DIM:HEAD_DIM + 1]))
            ms = jnp.mean(o * o, axis=0, keepdims=True)
            pair.append(o * lax.rsqrt(ms + 1e-5) * gain * (1.0 - lam_init))
        o_ref[0, :, p * PAIR:(p + 1) * PAIR] = jnp.concatenate(pair, axis=0).T.astype(BF16)


def _sb_scratch(tq):
    return [pltpu.VMEM((N_HEADS, tq, PAIR), BF16), pltpu.VMEM((N_HEADS, tq, 1), F32),
            pltpu.VMEM((N_HEADS, tq, PAIR), F32)]


def _sb_kernel(q_ref, k_ref, v_ref, o_ref, qs_ref, later_ref, acc_ref, *, tq, tk, off):
    i = pl.program_id(1)
    jd = (off + i * tq) // tk
    row = lax.broadcasted_iota(jnp.int32, (tk, tk), 0)
    col = lax.broadcasted_iota(jnp.int32, (tk, tk), 1)
    after = jnp.where(row > col, 1.0, 0.0).astype(BF16)
    qpos = off + i * tq + lax.broadcasted_iota(jnp.int32, (tq, tk), 0)
    kcol = lax.broadcasted_iota(jnp.int32, (tq, tk), 1)
    first = _head_lanes((tq, PAIR), 0, HEAD_DIM)

    for h in range(N_HEADS):
        qp = q_ref[0, :, (h // 2) * PAIR:(h // 2 + 1) * PAIR]
        qs_ref[h] = jnp.where(_head_lanes((tq, PAIR), (h % 2) * HEAD_DIM, HEAD_DIM), qp, jnp.zeros_like(qp))
    later_ref[...] = jnp.zeros(later_ref.shape, F32)
    acc_ref[...] = jnp.zeros(acc_ref.shape, F32)

    def block(j, diagonal):
        start = pl.multiple_of(j * tk, tk)
        for h in range(N_HEADS):
            p = h // 2
            kp = k_ref[0, pl.ds(start, tk), p * PAIR:(p + 1) * PAIR]
            vp = v_ref[0, pl.ds(start, tk), p * PAIR:(p + 1) * PAIR]
            z = _dot_nt(qs_ref[h], kp)
            t = jnp.log(1.0 + jnp.exp(-jnp.abs(z)))
            log_keep = -(jnp.maximum(z, 0.0) + t)
            log_beta = jnp.minimum(z, 0.0) - t
            if diagonal:
                earlier = (start + kcol) < qpos
                log_keep = jnp.where(earlier, log_keep, 0.0)
            hi = log_keep.astype(BF16)
            lo = (log_keep - hi.astype(F32)).astype(BF16)
            within = _dot(hi, after) + _dot(lo, after)
            w = jnp.exp(log_beta + within + later_ref[h])
            if diagonal:
                w = jnp.where(earlier, w, 0.0)
            acc_ref[h] += _dot(w.astype(BF16), vp)
            later_ref[h] += jnp.sum(log_keep, axis=-1, keepdims=True)
        return (jnp.max(later_ref[...]) > SB_DEAD).astype(jnp.int32)

    def body(c):
        j, _ = c
        return j - 1, block(j, False)

    lax.while_loop(lambda c: (c[0] >= 0) & (c[1] > 0), body, (jd - 1, block(jd, True)))
    for p in range(2):
        o_ref[0, :, p * PAIR:(p + 1) * PAIR] = jnp.where(first, acc_ref[2 * p], acc_ref[2 * p + 1]).astype(BF16)


def _causal_call(kernel, q, k, v, extra, extra_specs, tq, tk, off, name, scratch):
    b, t, qw = q.shape
    tk_all = k.shape[1]
    assert off % tk == 0 and tk % tq == 0 and t % tq == 0 and tk_all % tk == 0
    assert off + t <= tk_all
    return pl.pallas_call(
        functools.partial(kernel, tq=tq, tk=tk, off=off), grid=(b, t // tq),
        in_specs=[pl.BlockSpec((1, tq, qw), lambda b_, i: (b_, i, 0)),
                  _resident_spec(tk_all, k.shape[2]), _resident_spec(v.shape[1], v.shape[2])] + extra_specs,
        out_specs=pl.BlockSpec((1, tq, GROUP_WIDTH), lambda b_, i: (b_, i, 0)),
        out_shape=jax.ShapeDtypeStruct((b, t, GROUP_WIDTH), BF16),
        scratch_shapes=scratch,
        compiler_params=_params("parallel", "arbitrary"), name=name,
    )(q, k, v, *extra)


def _band_kernel(tab_ref, q_ref, k_ref, v_ref, o_ref, bias_ref, base_ref, *, tq, win, pos0):
    b = pl.program_id(0)
    i = pl.program_id(1)
    rows = 8
    wext = base_ref.shape[2]

    @pl.when((b == 0) & (i == 0))
    def _():
        rr = lax.broadcasted_iota(jnp.int32, (rows, wext), 0)
        x = lax.broadcasted_iota(jnp.int32, (rows, wext), 1)
        rel = jnp.clip(rr - x + tq + BAND_ROWS, -REL_CLIP, REL_CLIP) + REL_CLIP

        def pick(d, vals):
            hit = rel == d
            return tuple(jnp.where(hit, tab_ref[h, d], vals[h]) for h in range(N_HEADS))

        vals = lax.fori_loop(0, 2 * REL_CLIP + 1, pick,
                             tuple(jnp.zeros((rows, wext), F32) for _ in range(N_HEADS)))
        for h in range(N_HEADS):
            base_ref[h] = vals[h] * LOG2E

        def fill(c, carry):
            r0 = pl.multiple_of(c * rows, rows)
            r = r0 + lax.broadcasted_iota(jnp.int32, (rows, win), 0)
            u = lax.broadcasted_iota(jnp.int32, (rows, win), 1)
            dchunk = r // CHUNK - u // CHUNK + BAND_ROWS // CHUNK
            ok = (dchunk >= 0) & (dchunk <= BAND_ROWS // CHUNK)
            for h in range(N_HEADS):
                shifted = pltpu.roll(base_ref[h], wext - tq + r0, 1)[:, :win]
                bias_ref[h, pl.ds(r0, rows), :] = jnp.where(ok, shifted, NEG)
            return carry

        lax.fori_loop(0, tq // rows, fill, 0)

    start = pl.multiple_of(i * tq, tq)
    u = lax.broadcasted_iota(jnp.int32, (win, PAIR), 0)
    lane = lax.broadcasted_iota(jnp.int32, (win, PAIR), 1)
    missing = jnp.where(((pos0 + i * tq - BAND_ROWS + u) < 0) & (lane == 0), NEG, 0.0).astype(BF16)
    q_one = jnp.where(lax.broadcasted_iota(jnp.int32, (tq, PAIR), 1) == 0, 1.0, 0.0).astype(BF16)
    first = _head_lanes((tq, PAIR), 0, HEAD_DIM)

    def scores(h):
        p, hh = h // 2, h % 2
        qp = q_ref[0, :, p * PAIR:(p + 1) * PAIR]
        kw = jnp.concatenate([k_ref[0, pl.ds(start, win), p * PAIR:(p + 1) * PAIR], missing], axis=-1)
        qh = jnp.where(_head_lanes((tq, PAIR), hh * HEAD_DIM, HEAD_DIM), qp, jnp.zeros_like(qp))
        return _dot_nt(jnp.concatenate([qh, q_one], axis=-1), kw)

    outs = []
    s_next = scores(0)
    for h in range(N_HEADS):
        s = s_next + bias_ref[h]
        if h + 1 < N_HEADS:
            s_next = scores(h + 1)
        e = jnp.exp2(s - jnp.max(s, axis=-1, keepdims=True))
        vw = v_ref[0, pl.ds(start, win), (h // 2) * PAIR:(h // 2 + 1) * PAIR]
        outs.append(_dot(e.astype(BF16), vw) / jnp.sum(e, axis=-1, keepdims=True))
    for p in range(2):
        o_ref[0, :, p * PAIR:(p + 1) * PAIR] = jnp.where(first, outs[2 * p], outs[2 * p + 1]).astype(BF16)


def _band_call(q, k_pad, v_pad, rel_table, tq, win, pos0):
    b, t, _ = q.shape
    rows = k_pad.shape[1]
    assert t % tq == 0 and tq % CHUNK == 0 and win % LANES == 0 and win >= tq + BAND_ROWS
    assert rows >= t - tq + win and pos0 % CHUNK == 0
    return pl.pallas_call(
        functools.partial(_band_kernel, tq=tq, win=win, pos0=pos0), grid=(b, t // tq),
        in_specs=[pl.BlockSpec(memory_space=pltpu.SMEM),
                  pl.BlockSpec((1, tq, GROUP_WIDTH), lambda b_, i: (b_, i, 0)),
                  _resident_spec(rows, GROUP_WIDTH), _resident_spec(rows, GROUP_WIDTH)],
        out_specs=pl.BlockSpec((1, tq, GROUP_WIDTH), lambda b_, i: (b_, i, 0)),
        out_shape=jax.ShapeDtypeStruct((b, t, GROUP_WIDTH), BF16),
        scratch_shapes=[pltpu.VMEM((N_HEADS, tq, win), F32),
                        pltpu.VMEM((N_HEADS, 8, -(-(win + tq) // LANES) * LANES), F32)],
        compiler_params=_params("arbitrary", "arbitrary"), name="band",
    )(rel_table, q, k_pad, v_pad)


def _post_kernel(h_ref, oa_ref, ob_ref, oc_ref, od_ref, wout_ref, g1_ref, b1_ref,
                 wr_hi_ref, wr_lo_ref, br_ref, wg_ref, wu_ref, wd_ref, g2_ref, b2_ref, o_ref, *, alpha):
    tm = h_ref.shape[0]
    y = jnp.zeros(h_ref.shape, F32)
    for g, ref in enumerate((oa_ref, ob_ref, oc_ref, od_ref)):
        y = y + _dot(ref[...], wout_ref[g * GROUP_WIDTH:(g + 1) * GROUP_WIDTH, :])
    x = _layer_norm(alpha * h_ref[...] + y, g1_ref[...], b1_ref[...])

    x_hi = x.astype(BF16)
    x_lo = (x - x_hi.astype(F32)).astype(BF16)
    logit = (_dot(x_hi, wr_hi_ref[...]) + _dot(x_lo, wr_hi_ref[...]) + _dot(x_hi, wr_lo_ref[...])
             + br_ref[...])
    lane = lax.broadcasted_iota(jnp.int32, (tm, LANES), 1).astype(F32)
    ninf = -jnp.inf

    def lane_max(a):
        return jnp.max(a, axis=-1, keepdims=True)

    def first_lane(hit):
        return jnp.min(jnp.where(hit, lane, float(LANES)), axis=-1, keepdims=True)

    gl = jnp.where(lane < N_GROUPS, logit, ninf)
    g_max = lane_max(gl)
    g_idx = first_lane(gl == g_max)
    g_weight = 1.0 / jnp.sum(jnp.exp(gl - g_max), axis=-1, keepdims=True)
    e_lo = N_GROUPS + EXPERTS_PER_GROUP * g_idx
    el = jnp.where((lane >= e_lo) & (lane < e_lo + EXPERTS_PER_GROUP), logit, ninf)
    v1 = lane_max(el)
    i1 = first_lane(el == v1)
    el2 = jnp.where(lane == i1, ninf, el)
    v2 = lane_max(el2)
    i2 = first_lane(el2 == v2)
    e2 = jnp.exp(v2 - v1)
    w1 = g_weight / (1.0 + e2)
    w2 = g_weight * e2 / (1.0 + e2)
    comb = jnp.where(lane == i1, w1, 0.0) + jnp.where(lane == i2, w2, 0.0)

    xb = x_hi
    gate = _dot(xb, wg_ref[...])
    up = _dot(xb, wu_ref[...])
    hidden = []
    for e in range(N_EXPERTS):
        c_e = jnp.sum(jnp.where(lane == float(N_GROUPS + e), comb, 0.0), axis=-1, keepdims=True)
        ge = gate[:, e * D_EXPERT:(e + 1) * D_EXPERT]
        ue = up[:, e * D_EXPERT:(e + 1) * D_EXPERT]
        hidden.append(((ge * (1.0 / (1.0 + jnp.exp(-ge)))) * ue * c_e).astype(BF16))
    f = _dot(jnp.concatenate(hidden, axis=-1), wd_ref[...])
    o_ref[...] = _layer_norm(alpha * x + f, g2_ref[...], b2_ref[...])


def _post_call(h, mix, w, tm, alpha):
    n, d = h.shape
    de = N_EXPERTS * D_EXPERT
    return pl.pallas_call(
        functools.partial(_post_kernel, alpha=alpha), grid=(n // tm,),
        in_specs=[_row_spec(tm, d)] + [_row_spec(tm, GROUP_WIDTH)] * 4
        + [_const_spec((d, d)), _const_spec((1, d)), _const_spec((1, d)),
           _const_spec((d, LANES)), _const_spec((d, LANES)), _const_spec((1, LANES)),
           _const_spec((d, de)), _const_spec((d, de)), _const_spec((de, d)),
           _const_spec((1, d)), _const_spec((1, d))],
        out_specs=_row_spec(tm, d),
        out_shape=jax.ShapeDtypeStruct((n, d), F32),
        compiler_params=_params("parallel"), name="post",
    )(h, *mix, w["w_out"], w["ln1_g"], w["ln1_b"], w["wr_hi"], w["wr_lo"], w["b_r"],
      w["w_gate"], w["w_up"], w["w_down"], w["ln2_g"], w["ln2_b"])


def _wide_columns(w):
    rows = w.shape[0]
    w = w.reshape(rows, N_HEADS, HEAD_DIM)
    return jnp.concatenate([w, jnp.zeros_like(w)], axis=-1).reshape(rows, V_WIDE)


def _wide_values_t(v):
    one = jnp.ones(v.shape[:-1] + (1,), v.dtype)
    zero = jnp.zeros(v.shape[:-1] + (LANES - HEAD_DIM - 1,), v.dtype)
    wide = jnp.concatenate([v, one, zero], axis=-1).reshape(v.shape[0], v.shape[1], V_WIDE)
    return wide.transpose(0, 2, 1)


def _layer_weights(l, P):
    d = P["w_in"].shape[1]
    w_in = P["w_in"][l]
    kpe_lo = MLA_Q_LORA + MLA_KV_LORA
    dv_lo = kpe_lo + MLA_ROPE + 5 * GROUP_WIDTH
    w_in_pad = jnp.concatenate(
        [w_in[:, :kpe_lo], w_in[:, kpe_lo + MLA_ROPE:], w_in[:, kpe_lo:kpe_lo + MLA_ROPE],
         jnp.zeros((d, LANES - MLA_ROPE), F32)], axis=1)
    w_uq = P["mla_w_uq"][l]
    blocks = []
    for h in range(N_HEADS):
        src = h * (MLA_NOPE + MLA_ROPE)
        nope = w_uq[:, src:src + MLA_NOPE]
        zero = jnp.zeros_like(nope)
        blocks += ([nope, zero] if h % 2 == 0 else [zero, nope])
        blocks += [w_uq[:, src + MLA_NOPE:src + MLA_NOPE + MLA_ROPE],
                   jnp.zeros((MLA_Q_LORA, LANES - MLA_ROPE), F32)]
    w_r = jnp.concatenate([P["w_router_group"][l], P["w_router_expert"][l],
                           jnp.zeros((d, LANES - N_GROUPS - N_EXPERTS), F32)], axis=1)
    wr_hi = w_r.astype(BF16)
    b_r = jnp.concatenate([P["b_router_group"][l], P["b_router_expert"][l],
                           jnp.zeros((LANES - N_GROUPS - N_EXPERTS,), F32)]).reshape(1, LANES)
    de = N_EXPERTS * D_EXPERT
    return {
        "w_in": w_in_pad.astype(BF16),
        "q_norm": P["mla_q_norm"][l].reshape(1, -1),
        "w_uq": jnp.concatenate(blocks, axis=1).astype(BF16),
        "kv_norm": P["mla_kv_norm"][l].reshape(1, -1),
        "w_uk": P["mla_w_uk"][l].astype(BF16),
        "w_uv": _wide_columns(P["mla_w_uv"][l]).T.astype(BF16),
        "w_dv_t": _wide_columns(w_in[:, dv_lo:dv_lo + GROUP_WIDTH]).T.astype(BF16),
        "lam": jnp.stack([P["diff_lam_q1"][l], P["diff_lam_k1"][l],
                          P["diff_lam_q2"][l], P["diff_lam_k2"][l]]),
        "subln": P["diff_subln"][l].reshape(HEAD_DIM, 1),
        "rel": P["band_rel_bias"][l],
        "w_out": P["w_out"][l].astype(BF16),
        "ln1_g": P["ln1_g"][l].reshape(1, -1), "ln1_b": P["ln1_b"][l].reshape(1, -1),
        "wr_hi": wr_hi, "wr_lo": (w_r - wr_hi.astype(F32)).astype(BF16), "b_r": b_r,
        "w_gate": P["w_exp_gate"][l].transpose(2, 0, 1, 3).reshape(d, de).astype(BF16),
        "w_up": P["w_exp_up"][l].transpose(2, 0, 1, 3).reshape(d, de).astype(BF16),
        "w_down": P["w_exp_down"][l].reshape(de, d).astype(BF16),
        "ln2_g": P["ln2_g"][l].reshape(1, -1), "ln2_b": P["ln2_b"][l].reshape(1, -1),
    }


def _rope_tables(pos):
    inv_freq = ROPE_THETA ** (-jnp.arange(0, MLA_ROPE, 2, dtype=F32) / MLA_ROPE)
    ang = pos.astype(F32)[:, None] * inv_freq[None, :]
    cos, sin = jnp.cos(ang), jnp.sin(ang)
    reps = GROUP_WIDTH // MLA_ROPE
    return (jnp.tile(jnp.concatenate([cos, cos], axis=1), (1, reps)),
            jnp.tile(jnp.concatenate([-sin, sin], axis=1), (1, reps)))


def _flat_heads(a):
    return a.reshape(a.shape[0], a.shape[1], GROUP_WIDTH)


def _pad_rows(a, rows):
    return jnp.pad(a, ((0, 0), (0, rows - a.shape[1]), (0, 0)))


def _trunk(x, caches, P, weights):
    b, t, d = x.shape
    depth = len(weights)
    alpha = (2 * depth) ** 0.25
    n = b * t
    past = 0 if caches is None else caches[0].shape[2]
    if caches is None:
        tm, tq, tk, band_tq = 256, 512, 1024, 256
        sb_tq = sb_tk = 256
        tk_all = t
    else:
        tm, tq, band_tq = t, t, t
        tk = next(c for c in (512, 256, 2 * t) if past % c == 0)
        sb_tq, sb_tk = t, 2 * t
        tk_all = -(-(past + t) // tk) * tk
        assert caches[6].shape[2] == BAND_ROWS
    assert t >= BAND_ROWS or caches is not None
    band_win = -(-(band_tq + BAND_ROWS) // LANES) * LANES
    band_rows = t - band_tq + band_win
    cos, sin = _rope_tables(past + jnp.arange(t, dtype=jnp.int32))
    h = _ln_call(x.reshape(n, d), P["ln_in_g"], P["ln_in_b"], tm)
    new_rows = [[] for _ in range(8)]
    for l in range(depth):
        w = weights[l]
        f32, h16 = _pre_call(h, cos, sin, w, t, tm, BAND_ROWS if caches is None else 0)

        def new3(a):
            return a.reshape(b, t, a.shape[-1])

        if caches is None:
            def keys(name, cache_idx):
                return new3(h16[name])
            diff_v = h16["dv"]
            ckv_all, kpe_all = f32["ckv"], f32["kpe"]
            assert band_rows == BAND_ROWS + t
            band_k, band_v = h16["bk"], h16["bv"]
        else:
            c = [a[l] for a in caches]

            def keys(name, cache_idx):
                old = _flat_heads(c[cache_idx]).astype(BF16)
                return _pad_rows(jnp.concatenate([old, new3(h16[name])], axis=1), tk_all)
            diff_v = jnp.concatenate([_wide_values_t(c[5]).astype(BF16), h16["dv"]], axis=2)
            diff_v = jnp.pad(diff_v, ((0, 0), (0, 0), (0, tk_all - past - t)))
            ckv_all =_pad_rows(jnp.concatenate([c[0], new3(f32["ckv"])], axis=1), tk_all)
            ckv_all = ckv_all.reshape(b * tk_all, MLA_KV_LORA)
            kpe_old = jnp.pad(c[1], ((0, 0), (0, 0), (0, LANES - MLA_ROPE)))
            kpe_all = _pad_rows(jnp.concatenate([kpe_old, new3(f32["kpe"])], axis=1), tk_all)
            kpe_all = kpe_all.reshape(b * tk_all, LANES)
            band_k = _pad_rows(jnp.concatenate([_flat_heads(c[6]).astype(BF16), h16["bk"]], axis=1), band_rows)
            band_v = _pad_rows(jnp.concatenate([_flat_heads(c[7]).astype(BF16), h16["bv"]], axis=1), band_rows)

        km, vm = _kvup_call(ckv_all, kpe_all, w, tm if caches is None else tk, tk_all)
        o_a = _causal_call(_mla_kernel, new3(h16["qm"]), km.reshape(b, tk_all, -1), vm,
                           [], [], tq, tk, past, "mla", _flash_scratch(N_HEADS, tq))
        o_b = _causal_call(_sb_kernel, new3(h16["sbq"]), keys("sbk", 2), keys("sbv", 3), [], [],
                           sb_tq, sb_tk, past, "sb", _sb_scratch(sb_tq))
        lam_init = 0.8 - 0.6 * math.exp(-0.3 * l)
        o_c = _causal_call(functools.partial(_diff_kernel, lam_init=lam_init), new3(h16["dq"]),
                           keys("dk", 4), diff_v, [w["lam"], w["subln"]],
                           [_const_spec((4, DIFF_HALF)), _const_spec((HEAD_DIM, 1))], tq, tk, past, "diff",
                           _flash_scratch(2 * N_HEADS, tq))
        o_d = _band_call(new3(h16["bq"]), band_k, band_v, w["rel"], band_tq, band_win, past)
        mix = [o.reshape(n, GROUP_WIDTH) for o in (o_a, o_b, o_c, o_d)]
        h = _post_call(h, mix, w, tm, alpha)

        keep = min(BAND_ROWS, t)
        rows = (new3(f32["ckv"]), new3(f32["kpe"])[:, :, :MLA_ROPE],
                new3(f32["sbk"]), new3(f32["sbv"]), new3(f32["dk"]), new3(f32["dv"]),
                new3(f32["bk"])[:, t - keep:], new3(f32["bv"])[:, t - keep:])
        for i, (lst, arr) in enumerate(zip(new_rows, rows)):
            lst.append(arr if i < 2 else arr.reshape(b, arr.shape[1], N_HEADS, HEAD_DIM))
    return h.reshape(b, t, d), [jnp.stack(lst) for lst in new_rows]


def kernel(x_prompt, x_sample, cache_mla_ckv, cache_mla_kpe, cache_sb_k, cache_sb_v, cache_diff_k, cache_diff_v, cache_band_k, cache_band_v, ln_in_g, ln_in_b, w_in, mla_q_norm, mla_w_uq, mla_kv_norm, mla_w_uk, mla_w_uv, diff_lam_q1, diff_lam_k1, diff_lam_q2, diff_lam_k2, diff_subln, band_rel_bias, w_out, ln1_g, ln1_b, w_router_group, b_router_group, w_router_expert, b_router_expert, w_exp_gate, w_exp_up, w_exp_down, ln2_g, ln2_b):
    P = {
        "ln_in_g": ln_in_g, "ln_in_b": ln_in_b, "w_in": w_in,
        "mla_q_norm": mla_q_norm, "mla_w_uq": mla_w_uq, "mla_kv_norm": mla_kv_norm,
        "mla_w_uk": mla_w_uk, "mla_w_uv": mla_w_uv,
        "diff_lam_q1": diff_lam_q1, "diff_lam_k1": diff_lam_k1,
        "diff_lam_q2": diff_lam_q2, "diff_lam_k2": diff_lam_k2, "diff_subln": diff_subln,
        "band_rel_bias": band_rel_bias, "w_out": w_out, "ln1_g": ln1_g, "ln1_b": ln1_b,
        "w_router_group": w_router_group, "b_router_group": b_router_group,
        "w_router_expert": w_router_expert, "b_router_expert": b_router_expert,
        "w_exp_gate": w_exp_gate, "w_exp_up": w_exp_up, "w_exp_down": w_exp_down,
        "ln2_g": ln2_g, "ln2_b": ln2_b,
    }
    weights = [_layer_weights(l, P) for l in range(w_in.shape[0])]
    y_prompt, st_p = _trunk(x_prompt, None, P, weights)
    caches = (cache_mla_ckv, cache_mla_kpe, cache_sb_k, cache_sb_v,
              cache_diff_k, cache_diff_v, cache_band_k, cache_band_v)
    y_sample, st_s = _trunk(x_sample, caches, P, weights)
    return (y_prompt, y_sample, *st_p, *st_s)
```

```python
import functools
import math

import jax
import jax.numpy as jnp
from jax import lax
from jax.experimental import pallas as pl
from jax.experimental.pallas import tpu as pltpu

F32 = jnp.float32
BF16 = jnp.bfloat16

CHUNK = 64
HEAD_DIM = 64
N_HEADS = 4
GROUP_WIDTH = N_HEADS * HEAD_DIM
PAIR = 2 * HEAD_DIM
MLA_Q_LORA = 384
MLA_KV_LORA = 256
MLA_NOPE = 64
MLA_ROPE = 32
MLA_QBLK = 256
DIFF_HALF = 32
BAND_ROWS = 512
REL_CLIP = 128
N_GROUPS = 4
EXPERTS_PER_GROUP = 4
N_EXPERTS = N_GROUPS * EXPERTS_PER_GROUP
D_EXPERT = 128
ROPE_THETA = 10000.0
LN_EPS = 1e-5
NEG = -1e30
LANES = 128
SB_DEAD = -104.0
V_WIDE = N_HEADS * LANES
KPE_COL = MLA_Q_LORA + MLA_KV_LORA + 9 * GROUP_WIDTH
D_IN_PAD = KPE_COL + LANES
MLA_ONE_LANE = PAIR + MLA_ROPE
EXP_GUARD = 60.0
LOG2E = 1.4426950408889634
VMEM_LIMIT = 56 * 1024 * 1024

_NT = (((1,), (1,)), ((), ()))


def _params(*sem):
    return pltpu.CompilerParams(dimension_semantics=sem, vmem_limit_bytes=VMEM_LIMIT)


def _dot(a, b):
    return jnp.dot(a, b, preferred_element_type=F32)


def _dot_nt(a, b):
    return lax.dot_general(a, b, _NT, preferred_element_type=F32)


def _layer_norm(x, g, b):
    mu = jnp.mean(x, axis=-1, keepdims=True)
    var = jnp.mean(jnp.square(x - mu), axis=-1, keepdims=True)
    return (x - mu) * lax.rsqrt(var + LN_EPS) * g + b


def _rms_norm(x, w, eps):
    return x * lax.rsqrt(jnp.mean(jnp.square(x), axis=-1, keepdims=True) + eps) * w


def _rope(x, cos, sin_signed):
    n = x.shape[-1]
    lane = lax.broadcasted_iota(jnp.int32, x.shape, 1)
    partner = jnp.where((lane & 16) == 0, pltpu.roll(x, n - 16, 1), pltpu.roll(x, 16, 1))
    return x * cos + partner * sin_signed


def _ones_row(shape):
    row = lax.broadcasted_iota(jnp.int32, shape, 0)
    return jnp.where((row & (LANES - 1)) == HEAD_DIM, 1.0, 0.0)


def _head_lanes(shape, lo, width):
    lane = lax.broadcasted_iota(jnp.int32, shape, 1)
    return (lane >= lo) & (lane < lo + width)


def _row_spec(tm, width):
    return pl.BlockSpec((tm, width), lambda i: (i, 0))


def _const_spec(shape):
    return pl.BlockSpec(shape, lambda *_: (0,) * len(shape))


def _resident_spec(rows, width):
    return pl.BlockSpec((1, rows, width), lambda b, i: (b, 0, 0), pipeline_mode=pl.Buffered(1))


def _ln_kernel(x_ref, g_ref, b_ref, o_ref):
    o_ref[...] = _layer_norm(x_ref[...], g_ref[...], b_ref[...])


def _ln_call(x, g, b, tm):
    n, d = x.shape
    return pl.pallas_call(
        _ln_kernel, grid=(n // tm,),
        in_specs=[_row_spec(tm, d), _const_spec((1, d)), _const_spec((1, d))],
        out_specs=_row_spec(tm, d),
        out_shape=jax.ShapeDtypeStruct((n, d), F32),
        compiler_params=_params("parallel"), name="ln_in",
    )(x, g.reshape(1, d), b.reshape(1, d))


_PRE_F32 = ("ckv", "kpe", "sbk", "sbv", "dk", "dv", "bk", "bv")
_PRE_BF16 = ("sbq", "sbk", "sbv", "dq", "dk", "dv", "bq", "bk", "bv")


def _pre_kernel(x_ref, cos_ref, sin_ref, win_ref, qn_ref, wuq_ref, kvn_ref, wdvt_ref, bk_zero, bv_zero,
                ckv_o, kpe_o, sbk_o, sbv_o, dk_o, dv_o, bk_o, bv_o,
                qm_o, sbq_h, sbk_h, sbv_h, dq_h, dk_h, dv_h, bq_h, bk_h, bv_h):
    xb = x_ref[...].astype(BF16)
    cos = cos_ref[...]
    sin = sin_ref[...]
    cos_l, sin_l = cos[:, :LANES], sin[:, :LANES]

    def proj(seg, width=GROUP_WIDTH):
        return _dot(xb, win_ref[:, seg:seg + width])

    mla_scale = (MLA_NOPE + MLA_ROPE) ** -0.5 * LOG2E
    cq =_rms_norm(proj(0, MLA_Q_LORA), qn_ref[...], 1e-6)
    q = _dot(cq.astype(BF16), wuq_ref[...])
    for h in range(N_HEADS):
        lo = h * MLA_QBLK
        qm_o[:, lo:lo + LANES] = (q[:, lo:lo + LANES] * mla_scale).astype(BF16)
        qr = _rope(q[:, lo + LANES:lo + MLA_QBLK], cos_l, sin_l)
        qm_o[:, lo + LANES:lo + MLA_QBLK] = (qr * mla_scale).astype(BF16)
    seg = MLA_Q_LORA
    ckv_o[...] = _rms_norm(proj(seg), kvn_ref[...], 1e-6)
    seg += MLA_KV_LORA
    kpe_o[...] = _rope(proj(KPE_COL, LANES), cos_l, sin_l)

    sbq_h[...] = (proj(seg) * HEAD_DIM ** -0.5).astype(BF16)
    k = proj(seg + GROUP_WIDTH)
    sbk_o[...] = k
    sbk_h[...] = k.astype(BF16)
    v = proj(seg + 2 * GROUP_WIDTH)
    sbv_o[...] = v
    sbv_h[...] = v.astype(BF16)
    seg += 3 * GROUP_WIDTH
    dq_h[...] = (_rope(proj(seg), cos, sin) * (DIFF_HALF ** -0.5 * LOG2E)).astype(BF16)
    k = _rope(proj(seg + GROUP_WIDTH), cos, sin)
    dk_o[...] = k
    dk_h[...] = k.astype(BF16)
    dv_o[...] = proj(seg + 2 * GROUP_WIDTH)
    dv_h[0] = (_dot_nt(wdvt_ref[...], xb) + _ones_row(dv_h.shape[1:])).astype(BF16)
    seg += 3 * GROUP_WIDTH
    bq_h[...] = (proj(seg) * (HEAD_DIM ** -0.5 * LOG2E)).astype(BF16)
    k = proj(seg + GROUP_WIDTH)
    bk_o[...] = k
    bk_h[0] = k.astype(BF16)
    v = proj(seg + 2 * GROUP_WIDTH)
    bv_o[...] = v
    bv_h[0] = v.astype(BF16)


def _pre_call(x, cos, sin, w, t, tm, front):
    n, d = x.shape
    nt = t // tm
    assert front % tm == 0
    tab = pl.BlockSpec((tm, GROUP_WIDTH), lambda i: (i % nt, 0))
    widths_f32 = [GROUP_WIDTH, LANES] + [GROUP_WIDTH] * 6
    widths_h = [N_HEADS * MLA_QBLK] + [GROUP_WIDTH] * 9
    out_shape = ([jax.ShapeDtypeStruct((n, wd), F32) for wd in widths_f32]
                 + [jax.ShapeDtypeStruct((n, wd), BF16) for wd in widths_h])
    out_specs = [_row_spec(tm, wd) for wd in widths_f32 + widths_h]
    dv = len(widths_f32) + 1 + _PRE_BF16.index("dv")
    out_shape[dv] = jax.ShapeDtypeStruct((n // t, V_WIDE, t), BF16)
    out_specs[dv] = pl.BlockSpec((1, V_WIDE, tm), lambda i: (i // nt, 0, i % nt))
    band = [len(widths_f32) + 1 + _PRE_BF16.index(name) for name in ("bk", "bv")]
    padded = jax.ShapeDtypeStruct((n // t, front + t, GROUP_WIDTH), BF16)
    for o in band:
        out_shape[o] = padded
        out_specs[o] = pl.BlockSpec((1, tm, GROUP_WIDTH), lambda i: (i // nt, front // tm + i % nt, 0))
    inputs = (x, cos, sin, w["w_in"], w["q_norm"], w["w_uq"], w["kv_norm"], w["w_dv_t"])
    zeros = jnp.zeros(padded.shape, BF16)
    outs = pl.pallas_call(
        _pre_kernel, grid=(n // tm,),
        in_specs=[_row_spec(tm, d), tab, tab,
                  _const_spec((d, D_IN_PAD)), _const_spec((1, MLA_Q_LORA)),
                  _const_spec((MLA_Q_LORA, N_HEADS * MLA_QBLK)), _const_spec((1, MLA_KV_LORA)),
                  _const_spec((V_WIDE, d)), pl.BlockSpec(memory_space=pl.ANY), pl.BlockSpec(memory_space=pl.ANY)],
        out_specs=out_specs,
        out_shape=out_shape,
        input_output_aliases={len(inputs): band[0], len(inputs) + 1: band[1]},
        compiler_params=_params("parallel"), name="pre",
    )(*inputs, zeros, zeros)
    f32 = dict(zip(_PRE_F32, outs[:8]))
    h16 = dict(zip(("qm",) + _PRE_BF16, outs[8:]))
    return f32, h16


def _kvup_kernel(ckv_ref, kpe_ref, wuk_ref, wuv_ref, km_o, vm_o):
    cb = ckv_ref[...].astype(BF16)
    kn = _dot(cb, wuk_ref[...]).astype(BF16)
    lane = lax.broadcasted_iota(jnp.int32, kpe_ref.shape, 1)
    kpe = jnp.where(lane == MLA_ROPE, 1.0, kpe_ref[...]).astype(BF16)
    for p in range(2):
        km_o[:, p * MLA_QBLK:p * MLA_QBLK + PAIR] = kn[:, p * PAIR:(p + 1) * PAIR]
        km_o[:, p * MLA_QBLK + PAIR:(p + 1) * MLA_QBLK] = kpe
    vm_o[0] = (_dot_nt(wuv_ref[...], cb) + _ones_row(vm_o.shape[1:])).astype(BF16)


def _kvup_call(ckv, kpe, w, tm, rows):
    n = ckv.shape[0]
    nt = rows // tm
    return pl.pallas_call(
        _kvup_kernel, grid=(n // tm,),
        in_specs=[_row_spec(tm, MLA_KV_LORA), _row_spec(tm, LANES),
                  _const_spec((MLA_KV_LORA, GROUP_WIDTH)), _const_spec((V_WIDE, MLA_KV_LORA))],
        out_specs=[_row_spec(tm, 2 * MLA_QBLK), pl.BlockSpec((1, V_WIDE, tm), lambda i: (i // nt, 0, i % nt))],
        out_shape=[jax.ShapeDtypeStruct((n, 2 * MLA_QBLK), BF16),
                   jax.ShapeDtypeStruct((n // rows, V_WIDE, rows), BF16)],
        compiler_params=_params("parallel"), name="kvup",
    )(ckv, kpe, w["w_uk"], w["w_uv"])


def _fold_max(d):
    m = d[:, :LANES]
    for c in range(1, d.shape[1] // LANES):
        m = jnp.maximum(m, d[:, c * LANES:(c + 1) * LANES])
    return jnp.max(m.reshape(m.shape[0] // 8, 8, m.shape[1]), axis=0)


def _softmax_chains(chains, operands, i, jd, tq, tk, off, qs_ref, acc_ref):
    sub = max(tq, LANES)
    qstart = off + i * tq
    own = pl.multiple_of(qstart // sub * sub, sub)
    own_mask = ((own + lax.broadcasted_iota(jnp.int32, (tq, sub), 1)) // CHUNK
                <= (qstart + lax.broadcasted_iota(jnp.int32, (tq, sub), 0)) // CHUNK)
    own_mask_t = ((own + lax.broadcasted_iota(jnp.int32, (sub, tq), 0)) // CHUNK
                  <= (qstart + lax.broadcasted_iota(jnp.int32, (sub, tq), 1)) // CHUNK)

    def sweep(step, carry):
        carry = lax.fori_loop(0, jd, lambda j, cy: step(pl.multiple_of(j * tk, tk), tk, False, cy), carry)
        carry = lax.fori_loop(0, (own - jd * tk) // sub,
                              lambda s, cy: step(pl.multiple_of(jd * tk + s * sub, sub), sub, False, cy), carry)
        return step(own, sub, True, carry)

    for n in range(chains):
        q, k, _ = operands(n, own, sub)
        m0 = jnp.max(jnp.where(own_mask, _dot_nt(q, k), NEG), axis=-1, keepdims=True)
        lane = lax.broadcasted_iota(jnp.int32, q.shape, 1)
        qs_ref[n] = jnp.where(lane == operands(n, None), (-m0).astype(BF16), q)
    acc_ref[...] = jnp.zeros(acc_ref.shape, F32)

    def fast(start, rows, masked, top):
        d_next = _dot_nt(operands(0, start, rows)[1], qs_ref[0])
        for n in range(chains):
            d = d_next
            if n + 1 < chains:
                d_next = _dot_nt(operands(n + 1, start, rows)[1], qs_ref[n + 1])
            if masked:
                d = jnp.where(own_mask_t, d, NEG)
            top = jnp.maximum(top, _fold_max(d))
            acc_ref[n] += _dot(operands(n, start, rows)[2], jnp.exp2(d).astype(BF16))
        return top

    top = sweep(fast, jnp.full((8, min(tq, LANES)), NEG, F32))

    @pl.when(jnp.max(top) > EXP_GUARD)
    def _():
        def safe(start, rows, masked, states):
            new = []
            for n in range(chains):
                q, k, vt = operands(n, start, rows)
                s = _dot_nt(k, q)
                if masked:
                    s = jnp.where(own_mask_t, s, NEG)
                m, acc = states[n]
                m_new = jnp.maximum(m, jnp.max(s, axis=0, keepdims=True))
                new.append((m_new, jnp.exp2(m - m_new) * acc + _dot(vt, jnp.exp2(s - m_new).astype(BF16))))
            return tuple(new)

        states = sweep(safe, tuple((jnp.full((1, tq), NEG, F32), jnp.zeros((LANES, tq), F32))
                                   for _ in range(chains)))
        for n in range(chains):
            acc_ref[n] = states[n][1]


def _flash_scratch(chains, tq):
    return [pltpu.VMEM((chains, tq, MLA_QBLK), BF16), pltpu.VMEM((chains, LANES, tq), F32)]


def _mla_kernel(q_ref, k_ref, v_ref, o_ref, qs_ref, acc_ref, *, tq, tk, off):
    i = pl.program_id(1)
    jd = (off + i * tq) // tk

    def operands(h, start, rows=tk):
        if start is None:
            return MLA_ONE_LANE
        p = h // 2
        return (q_ref[0, :, h * MLA_QBLK:(h + 1) * MLA_QBLK],
                k_ref[0, pl.ds(start, rows), p * MLA_QBLK:(p + 1) * MLA_QBLK],
                v_ref[0, h * LANES:(h + 1) * LANES, pl.ds(start, rows)])

    _softmax_chains(N_HEADS, operands, i, jd, tq, tk, off, qs_ref, acc_ref)
    for p in range(2):
        pair = []
        for h in (2 * p, 2 * p + 1):
            acc = acc_ref[h]
            pair.append(acc[:HEAD_DIM] / acc[HEAD_DIM:HEAD_DIM + 1])
        o_ref[0, :, p * PAIR:(p + 1) * PAIR] = jnp.concatenate(pair, axis=0).T.astype(BF16)


def _diff_kernel(q_ref, k_ref, v_ref, lam_ref, subln_ref, o_ref, qs_ref, acc_ref, *, tq, tk, off, lam_init):
    i = pl.program_id(1)
    jd = (off + i * tq) // tk

    def operands(n, start, rows=tk):
        if start is None:
            return PAIR
        h, half = n // 2, n % 2
        p, hh = h // 2, h % 2
        qp = q_ref[0, :, p * PAIR:(p + 1) * PAIR]
        sel = _head_lanes((tq, PAIR), hh * HEAD_DIM + half * DIFF_HALF, DIFF_HALF)
        zero = jnp.zeros_like(qp)
        one = jnp.where(lax.broadcasted_iota(jnp.int32, (rows, PAIR), 1) == 0, 1.0, 0.0).astype(BF16)
        return (jnp.concatenate([jnp.where(sel, qp, zero), zero], axis=-1),
                jnp.concatenate([k_ref[0, pl.ds(start, rows), p * PAIR:(p + 1) * PAIR], one], axis=-1),
                v_ref[0, h * LANES:(h + 1) * LANES, pl.ds(start, rows)])

    _softmax_chains(2 * N_HEADS, operands, i, jd, tq, tk, off, qs_ref, acc_ref)

    lp = lam_ref[...]
    lam = (jnp.exp(jnp.sum(lp[0:1] * lp[1:2], axis=-1, keepdims=True))
           - jnp.exp(jnp.sum(lp[2:3] * lp[3:4], axis=-1, keepdims=True)) + lam_init)
    gain = subln_ref[...]
    for p in range(2):
        pair = []
        for h in (2 * p, 2 * p + 1):
            a1, a2 = acc_ref[2 * h], acc_ref[2 * h + 1]
            o = (a1[:HEAD_DIM] / a1[HEAD_DIM:HEAD_DIM + 1]
                 - lam * (a2[:HEAD_DIM] / a2[HEAD_DIM:HEAD_DIM + 1]))
            ms = jnp.mean(o * o, axis=0, keepdims=True)
            pair.append(o * lax.rsqrt(ms + 1e-5) * gain * (1.0 - lam_init))
        o_ref[0, :, p * PAIR:(p + 1) * PAIR] = jnp.concatenate(pair, axis=0).T.astype(BF16)


def _sb_scratch(tq):
    return [pltpu.VMEM((N_HEADS, tq, PAIR), BF16), pltpu.VMEM((N_HEADS, tq, 1), F32),
            pltpu.VMEM((N_HEADS, tq, PAIR), F32)]


def _sb_kernel(q_ref, k_ref, v_ref, o_ref, qs_ref, later_ref, acc_ref, *, tq, tk, off):
    i = pl.program_id(1)
    jd = (off + i * tq) // tk
    row = lax.broadcasted_iota(jnp.int32, (tk, tk), 0)
    col = lax.broadcasted_iota(jnp.int32, (tk, tk), 1)
    after = jnp.where(row > col, 1.0, 0.0).astype(BF16)
    qpos = off + i * tq + lax.broadcasted_iota(jnp.int32, (tq, tk), 0)
    kcol = lax.broadcasted_iota(jnp.int32, (tq, tk), 1)
    first = _head_lanes((tq, PAIR), 0, HEAD_DIM)

    for h in range(N_HEADS):
        qp = q_ref[0, :, (h // 2) * PAIR:(h // 2 + 1) * PAIR]
        qs_ref[h] = jnp.where(_head_lanes((tq, PAIR), (h % 2) * HEAD_DIM, HEAD_DIM), qp, jnp.zeros_like(qp))
    later_ref[...] = jnp.zeros(later_ref.shape, F32)
    acc_ref[...] = jnp.zeros(acc_ref.shape, F32)

    def block(j, diagonal):
        start = pl.multiple_of(j * tk, tk)

        def logits(h):
            return _dot_nt(qs_ref[h], k_ref[0, pl.ds(start, tk), (h // 2) * PAIR:(h // 2 + 1) * PAIR])

        z_next = logits(0)
        for h in range(N_HEADS):
            vp = v_ref[0, pl.ds(start, tk), (h // 2) * PAIR:(h // 2 + 1) * PAIR]
            z = z_next
            if h + 1 < N_HEADS:
                z_next = logits(h + 1)
            t = jnp.log(1.0 + jnp.exp(-jnp.abs(z)))
            log_keep = -(jnp.maximum(z, 0.0) + t)
            log_beta = jnp.minimum(z, 0.0) - t
            if diagonal:
                earlier = (start + kcol) < qpos
                log_keep = jnp.where(earlier, log_keep, 0.0)
            hi = log_keep.astype(BF16)
            lo = (log_keep - hi.astype(F32)).astype(BF16)
            within = _dot(hi, after) + _dot(lo, after)
            w = jnp.exp(log_beta + within + later_ref[h])
            if diagonal:
                w = jnp.where(earlier, w, 0.0)
            acc_ref[h] += _dot(w.astype(BF16), vp)
            later_ref[h] += jnp.sum(log_keep, axis=-1, keepdims=True)
        return (jnp.max(later_ref[...]) > SB_DEAD).astype(jnp.int32)

    def body(c):
        j, _ = c
        return j - 1, block(j, False)

    lax.while_loop(lambda c: (c[0] >= 0) & (c[1] > 0), body, (jd - 1, block(jd, True)))
    for p in range(2):
        o_ref[0, :, p * PAIR:(p + 1) * PAIR] = jnp.where(first, acc_ref[2 * p], acc_ref[2 * p + 1]).astype(BF16)


def _causal_call(kernel, q, k, v, extra, extra_specs, tq, tk, off, name, scratch):
    b, t, qw = q.shape
    tk_all = k.shape[1]
    assert off % tk == 0 and tk % tq == 0 and t % tq == 0 and tk_all % tk == 0
    assert off + t <= tk_all
    return pl.pallas_call(
        functools.partial(kernel, tq=tq, tk=tk, off=off), grid=(b, t // tq),
        in_specs=[pl.BlockSpec((1, tq, qw), lambda b_, i: (b_, i, 0)),
                  _resident_spec(tk_all, k.shape[2]), _resident_spec(v.shape[1], v.shape[2])] + extra_specs,
        out_specs=pl.BlockSpec((1, tq, GROUP_WIDTH), lambda b_, i: (b_, i, 0)),
        out_shape=jax.ShapeDtypeStruct((b, t, GROUP_WIDTH), BF16),
        scratch_shapes=scratch,
        compiler_params=_params("parallel", "arbitrary"), name=name,
    )(q, k, v, *extra)


def _band_kernel(tab_ref, q_ref, k_ref, v_ref, o_ref, bias_ref, base_ref, *, tq, win, pos0):
    b = pl.program_id(0)
    i = pl.program_id(1)
    rows = 8
    wext = base_ref.shape[2]

    @pl.when((b == 0) & (i == 0))
    def _():
        rr = lax.broadcasted_iota(jnp.int32, (rows, wext), 0)
        x = lax.broadcasted_iota(jnp.int32, (rows, wext), 1)
        rel = jnp.clip(rr - x + tq + BAND_ROWS, -REL_CLIP, REL_CLIP) + REL_CLIP

        def pick(d, vals):
            hit = rel == d
            return tuple(jnp.where(hit, tab_ref[h, d], vals[h]) for h in range(N_HEADS))

        vals = lax.fori_loop(0, 2 * REL_CLIP + 1, pick,
                             tuple(jnp.zeros((rows, wext), F32) for _ in range(N_HEADS)))
        for h in range(N_HEADS):
            base_ref[h] = vals[h] * LOG2E

        def fill(c, carry):
            r0 = pl.multiple_of(c * rows, rows)
            r = r0 + lax.broadcasted_iota(jnp.int32, (rows, win), 0)
            u = lax.broadcasted_iota(jnp.int32, (rows, win), 1)
            dchunk = r // CHUNK - u // CHUNK + BAND_ROWS // CHUNK
            ok = (dchunk >= 0) & (dchunk <= BAND_ROWS // CHUNK)
            for h in range(N_HEADS):
                shifted = pltpu.roll(base_ref[h], wext - tq + r0, 1)[:, :win]
                bias_ref[h, pl.ds(r0, rows), :] = jnp.where(ok, shifted, NEG)
            return carry

        lax.fori_loop(0, tq // rows, fill, 0)

    start = pl.multiple_of(i * tq, tq)
    u = lax.broadcasted_iota(jnp.int32, (win, PAIR), 0)
    lane = lax.broadcasted_iota(jnp.int32, (win, PAIR), 1)
    missing = jnp.where(((pos0 + i * tq - BAND_ROWS + u) < 0) & (lane == 0), NEG, 0.0).astype(BF16)
    q_one = jnp.where(lax.broadcasted_iota(jnp.int32, (tq, PAIR), 1) == 0, 1.0, 0.0).astype(BF16)
    first = _head_lanes((tq, PAIR), 0, HEAD_DIM)

    def scores(h):
        p, hh = h // 2, h % 2
        qp = q_ref[0, :, p * PAIR:(p + 1) * PAIR]
        kw = jnp.concatenate([k_ref[0, pl.ds(start, win), p * PAIR:(p + 1) * PAIR], missing], axis=-1)
        qh = jnp.where(_head_lanes((tq, PAIR), hh * HEAD_DIM, HEAD_DIM), qp, jnp.zeros_like(qp))
        return _dot_nt(jnp.concatenate([qh, q_one], axis=-1), kw)

    outs = []
    s_next = scores(0)
    for h in range(N_HEADS):
        s = s_next + bias_ref[h]
        if h + 1 < N_HEADS:
            s_next = scores(h + 1)
        e = jnp.exp2(s - jnp.max(s, axis=-1, keepdims=True))
        vw = v_ref[0, pl.ds(start, win), (h // 2) * PAIR:(h // 2 + 1) * PAIR]
        outs.append(_dot(e.astype(BF16), vw) / jnp.sum(e, axis=-1, keepdims=True))
    for p in range(2):
        o_ref[0, :, p * PAIR:(p + 1) * PAIR] = jnp.where(first, outs[2 * p], outs[2 * p + 1]).astype(BF16)


def _band_call(q, k_pad, v_pad, rel_table, tq, win, pos0):
    b, t, _ = q.shape
    rows = k_pad.shape[1]
    assert t % tq == 0 and tq % CHUNK == 0 and win % LANES == 0 and win >= tq + BAND_ROWS
    assert rows >= t - tq + win and pos0 % CHUNK == 0
    return pl.pallas_call(
        functools.partial(_band_kernel, tq=tq, win=win, pos0=pos0), grid=(b, t // tq),
        in_specs=[pl.BlockSpec(memory_space=pltpu.SMEM),
                  pl.BlockSpec((1, tq, GROUP_WIDTH), lambda b_, i: (b_, i, 0)),
                  _resident_spec(rows, GROUP_WIDTH), _resident_spec(rows, GROUP_WIDTH)],
        out_specs=pl.BlockSpec((1, tq, GROUP_WIDTH), lambda b_, i: (b_, i, 0)),
        out_shape=jax.ShapeDtypeStruct((b, t, GROUP_WIDTH), BF16),
        scratch_shapes=[pltpu.VMEM((N_HEADS, tq, win), F32),
                        pltpu.VMEM((N_HEADS, 8, -(-(win + tq) // LANES) * LANES), F32)],
        compiler_params=_params("arbitrary", "arbitrary"), name="band",
    )(rel_table, q, k_pad, v_pad)


def _post_kernel(h_ref, oa_ref, ob_ref, oc_ref, od_ref, wout_ref, g1_ref, b1_ref,
                 wr_hi_ref, wr_lo_ref, br_ref, wg_ref, wu_ref, wd_ref, g2_ref, b2_ref, o_ref, *, alpha):
    tm = h_ref.shape[0]
    y = jnp.zeros(h_ref.shape, F32)
    for g, ref in enumerate((oa_ref, ob_ref, oc_ref, od_ref)):
        y = y + _dot(ref[...], wout_ref[g * GROUP_WIDTH:(g + 1) * GROUP_WIDTH, :])
    x = _layer_norm(alpha * h_ref[...] + y, g1_ref[...], b1_ref[...])

    x_hi = x.astype(BF16)
    x_lo = (x - x_hi.astype(F32)).astype(BF16)
    logit = (_dot(x_hi, wr_hi_ref[...]) + _dot(x_lo, wr_hi_ref[...]) + _dot(x_hi, wr_lo_ref[...])
             + br_ref[...])
    lane = lax.broadcasted_iota(jnp.int32, (tm, LANES), 1).astype(F32)
    ninf = -jnp.inf

    def lane_max(a):
        return jnp.max(a, axis=-1, keepdims=True)

    def first_lane(hit):
        return jnp.min(jnp.where(hit, lane, float(LANES)), axis=-1, keepdims=True)

    gl = jnp.where(lane < N_GROUPS, logit, ninf)
    g_max = lane_max(gl)
    g_idx = first_lane(gl == g_max)
    g_weight = 1.0 / jnp.sum(jnp.exp(gl - g_max), axis=-1, keepdims=True)
    e_lo = N_GROUPS + EXPERTS_PER_GROUP * g_idx
    el = jnp.where((lane >= e_lo) & (lane < e_lo + EXPERTS_PER_GROUP), logit, ninf)
    v1 = lane_max(el)
    i1 = first_lane(el == v1)
    el2 = jnp.where(lane == i1, ninf, el)
    v2 = lane_max(el2)
    i2 = first_lane(el2 == v2)
    e2 = jnp.exp(v2 - v1)
    w1 = g_weight / (1.0 + e2)
    w2 = g_weight * e2 / (1.0 + e2)
    comb = jnp.where(lane == i1, w1, 0.0) + jnp.where(lane == i2, w2, 0.0)

    xb = x_hi
    gate = _dot(xb, wg_ref[...])
    up = _dot(xb, wu_ref[...])
    hidden = []
    for e in range(N_EXPERTS):
        c_e = jnp.sum(jnp.where(lane == float(N_GROUPS + e), comb, 0.0), axis=-1, keepdims=True)
        ge = gate[:, e * D_EXPERT:(e + 1) * D_EXPERT]
        ue = up[:, e * D_EXPERT:(e + 1) * D_EXPERT]
        hidden.append(((ge * (1.0 / (1.0 + jnp.exp(-ge)))) * ue * c_e).astype(BF16))
    f = _dot(jnp.concatenate(hidden, axis=-1), wd_ref[...])
    o_ref[...] = _layer_norm(alpha * x + f, g2_ref[...], b2_ref[...])


def _post_call(h, mix, w, tm, alpha):
    n, d = h.shape
    de = N_EXPERTS * D_EXPERT
    return pl.pallas_call(
        functools.partial(_post_kernel, alpha=alpha), grid=(n // tm,),
        in_specs=[_row_spec(tm, d)] + [_row_spec(tm, GROUP_WIDTH)] * 4
        + [_const_spec((d, d)), _const_spec((1, d)), _const_spec((1, d)),
           _const_spec((d, LANES)), _const_spec((d, LANES)), _const_spec((1, LANES)),
           _const_spec((d, de)), _const_spec((d, de)), _const_spec((de, d)),
           _const_spec((1, d)), _const_spec((1, d))],
        out_specs=_row_spec(tm, d),
        out_shape=jax.ShapeDtypeStruct((n, d), F32),
        compiler_params=_params("parallel"), name="post",
    )(h, *mix, w["w_out"], w["ln1_g"], w["ln1_b"], w["wr_hi"], w["wr_lo"], w["b_r"],
      w["w_gate"], w["w_up"], w["w_down"], w["ln2_g"], w["ln2_b"])


def _wide_columns(w):
    rows = w.shape[0]
    w = w.reshape(rows, N_HEADS, HEAD_DIM)
    return jnp.concatenate([w, jnp.zeros_like(w)], axis=-1).reshape(rows, V_WIDE)


def _wide_values_t(v):
    one = jnp.ones(v.shape[:-1] + (1,), v.dtype)
    zero = jnp.zeros(v.shape[:-1] + (LANES - HEAD_DIM - 1,), v.dtype)
    wide = jnp.concatenate([v, one, zero], axis=-1).reshape(v.shape[0], v.shape[1], V_WIDE)
    return wide.transpose(0, 2, 1)


def _layer_weights(l, P):
    d = P["w_in"].shape[1]
    w_in = P["w_in"][l]
    kpe_lo = MLA_Q_LORA + MLA_KV_LORA
    dv_lo = kpe_lo + MLA_ROPE + 5 * GROUP_WIDTH
    w_in_pad = jnp.concatenate(
        [w_in[:, :kpe_lo], w_in[:, kpe_lo + MLA_ROPE:], w_in[:, kpe_lo:kpe_lo + MLA_ROPE],
         jnp.zeros((d, LANES - MLA_ROPE), F32)], axis=1)
    w_uq = P["mla_w_uq"][l]
    blocks = []
    for h in range(N_HEADS):
        src = h * (MLA_NOPE + MLA_ROPE)
        nope = w_uq[:, src:src + MLA_NOPE]
        zero = jnp.zeros_like(nope)
        blocks += ([nope, zero] if h % 2 == 0 else [zero, nope])
        blocks += [w_uq[:, src + MLA_NOPE:src + MLA_NOPE + MLA_ROPE],
                   jnp.zeros((MLA_Q_LORA, LANES - MLA_ROPE), F32)]
    w_r = jnp.concatenate([P["w_router_group"][l], P["w_router_expert"][l],
                           jnp.zeros((d, LANES - N_GROUPS - N_EXPERTS), F32)], axis=1)
    wr_hi = w_r.astype(BF16)
    b_r = jnp.concatenate([P["b_router_group"][l], P["b_router_expert"][l],
                           jnp.zeros((LANES - N_GROUPS - N_EXPERTS,), F32)]).reshape(1, LANES)
    de = N_EXPERTS * D_EXPERT
    return {
        "w_in": w_in_pad.astype(BF16),
        "q_norm": P["mla_q_norm"][l].reshape(1, -1),
        "w_uq": jnp.concatenate(blocks, axis=1).astype(BF16),
        "kv_norm": P["mla_kv_norm"][l].reshape(1, -1),
        "w_uk": P["mla_w_uk"][l].astype(BF16),
        "w_uv": _wide_columns(P["mla_w_uv"][l]).T.astype(BF16),
        "w_dv_t": _wide_columns(w_in[:, dv_lo:dv_lo + GROUP_WIDTH]).T.astype(BF16),
        "lam": jnp.stack([P["diff_lam_q1"][l], P["diff_lam_k1"][l],
                          P["diff_lam_q2"][l], P["diff_lam_k2"][l]]),
        "subln": P["diff_subln"][l].reshape(HEAD_DIM, 1),
        "rel": P["band_rel_bias"][l],
        "w_out": P["w_out"][l].astype(BF16),
        "ln1_g": P["ln1_g"][l].reshape(1, -1), "ln1_b": P["ln1_b"][l].reshape(1, -1),
        "wr_hi": wr_hi, "wr_lo": (w_r - wr_hi.astype(F32)).astype(BF16), "b_r": b_r,
        "w_gate": P["w_exp_gate"][l].transpose(2, 0, 1, 3).reshape(d, de).astype(BF16),
        "w_up": P["w_exp_up"][l].transpose(2, 0, 1, 3).reshape(d, de).astype(BF16),
        "w_down": P["w_exp_down"][l].reshape(de, d).astype(BF16),
        "ln2_g": P["ln2_g"][l].reshape(1, -1), "ln2_b": P["ln2_b"][l].reshape(1, -1),
    }


def _rope_tables(pos):
    inv_freq = ROPE_THETA ** (-jnp.arange(0, MLA_ROPE, 2, dtype=F32) / MLA_ROPE)
    ang = pos.astype(F32)[:, None] * inv_freq[None, :]
    cos, sin = jnp.cos(ang), jnp.sin(ang)
    reps = GROUP_WIDTH // MLA_ROPE
    return (jnp.tile(jnp.concatenate([cos, cos], axis=1), (1, reps)),
            jnp.tile(jnp.concatenate([-sin, sin], axis=1), (1, reps)))


def _flat_heads(a):
    return a.reshape(a.shape[0], a.shape[1], GROUP_WIDTH)


def _pad_rows(a, rows):
    return jnp.pad(a, ((0, 0), (0, rows - a.shape[1]), (0, 0)))


def _trunk(x, caches, P, weights):
    b, t, d = x.shape
    depth = len(weights)
    alpha = (2 * depth) ** 0.25
    n = b * t
    past = 0 if caches is None else caches[0].shape[2]
    if caches is None:
        tm, tq, tk, band_tq = 256, 512, 1024, 256
        sb_tq = sb_tk = 256
        tk_all = t
    else:
        tm, tq, band_tq = t, t, t
        tk = next(c for c in (512, 256, 2 * t) if past % c == 0)
        sb_tq, sb_tk = t, 2 * t
        tk_all = -(-(past + t) // tk) * tk
        assert caches[6].shape[2] == BAND_ROWS
    assert t >= BAND_ROWS or caches is not None
    band_win = -(-(band_tq + BAND_ROWS) // LANES) * LANES
    band_rows = t - band_tq + band_win
    cos, sin = _rope_tables(past + jnp.arange(t, dtype=jnp.int32))
    h = _ln_call(x.reshape(n, d), P["ln_in_g"], P["ln_in_b"], tm)
    new_rows = [[] for _ in range(8)]
    for l in range(depth):
        w = weights[l]
        f32, h16 = _pre_call(h, cos, sin, w, t, tm, BAND_ROWS if caches is None else 0)

        def new3(a):
            return a.reshape(b, t, a.shape[-1])

        if caches is None:
            def keys(name, cache_idx):
                return new3(h16[name])
            diff_v = h16["dv"]
            ckv_all, kpe_all = f32["ckv"], f32["kpe"]
            assert band_rows == BAND_ROWS + t
            band_k, band_v = h16["bk"], h16["bv"]
        else:
            c = [a[l] for a in caches]

            def keys(name, cache_idx):
                old = _flat_heads(c[cache_idx]).astype(BF16)
                return _pad_rows(jnp.concatenate([old, new3(h16[name])], axis=1), tk_all)
            diff_v = jnp.concatenate([_wide_values_t(c[5]).astype(BF16), h16["dv"]], axis=2)
            diff_v = jnp.pad(diff_v, ((0, 0), (0, 0), (0, tk_all - past - t)))
            ckv_all =_pad_rows(jnp.concatenate([c[0], new3(f32["ckv"])], axis=1), tk_all)
            ckv_all = ckv_all.reshape(b * tk_all, MLA_KV_LORA)
            kpe_old = jnp.pad(c[1], ((0, 0), (0, 0), (0, LANES - MLA_ROPE)))
            kpe_all = _pad_rows(jnp.concatenate([kpe_old, new3(f32["kpe"])], axis=1), tk_all)
            kpe_all = kpe_all.reshape(b * tk_all, LANES)
            band_k = _pad_rows(jnp.concatenate([_flat_heads(c[6]).astype(BF16), h16["bk"]], axis=1), band_rows)
            band_v = _pad_rows(jnp.concatenate([_flat_heads(c[7]).astype(BF16), h16["bv"]], axis=1), band_rows)

        km, vm = _kvup_call(ckv_all, kpe_all, w, tm if caches is None else tk, tk_all)
        o_a = _causal_call(_mla_kernel, new3(h16["qm"]), km.reshape(b, tk_all, -1), vm,
                           [], [], tq, tk, past, "mla", _flash_scratch(N_HEADS, tq))
        o_b = _causal_call(_sb_kernel, new3(h16["sbq"]), keys("sbk", 2), keys("sbv", 3), [], [],
                           sb_tq, sb_tk, past, "sb", _sb_scratch(sb_tq))
        lam_init = 0.8 - 0.6 * math.exp(-0.3 * l)
        o_c = _causal_call(functools.partial(_diff_kernel, lam_init=lam_init), new3(h16["dq"]),
                           keys("dk", 4), diff_v, [w["lam"], w["subln"]],
                           [_const_spec((4, DIFF_HALF)), _const_spec((HEAD_DIM, 1))], tq, tk, past, "diff",
                           _flash_scratch(2 * N_HEADS, tq))
        o_d = _band_call(new3(h16["bq"]), band_k, band_v, w["rel"], band_tq, band_win, past)
        mix = [o.reshape(n, GROUP_WIDTH) for o in (o_a, o_b, o_c, o_d)]
        h = _post_call(h, mix, w, tm, alpha)

        keep = min(BAND_ROWS, t)
        rows = (new3(f32["ckv"]), new3(f32["kpe"])[:, :, :MLA_ROPE],
                new3(f32["sbk"]), new3(f32["sbv"]), new3(f32["dk"]), new3(f32["dv"]),
                new3(f32["bk"])[:, t - keep:], new3(f32["bv"])[:, t - keep:])
        for i, (lst, arr) in enumerate(zip(new_rows, rows)):
            lst.append(arr if i < 2 else arr.reshape(b, arr.shape[1], N_HEADS, HEAD_DIM))
    return h.reshape(b, t, d), [jnp.stack(lst) for lst in new_rows]


def kernel(x_prompt, x_sample, cache_mla_ckv, cache_mla_kpe, cache_sb_k, cache_sb_v, cache_diff_k, cache_diff_v, cache_band_k, cache_band_v, ln_in_g, ln_in_b, w_in, mla_q_norm, mla_w_uq, mla_kv_norm, mla_w_uk, mla_w_uv, diff_lam_q1, diff_lam_k1, diff_lam_q2, diff_lam_k2, diff_subln, band_rel_bias, w_out, ln1_g, ln1_b, w_router_group, b_router_group, w_router_expert, b_router_expert, w_exp_gate, w_exp_up, w_exp_down, ln2_g, ln2_b):
    P = {
        "ln_in_g": ln_in_g, "ln_in_b": ln_in_b, "w_in": w_in,
        "mla_q_norm": mla_q_norm, "mla_w_uq": mla_w_uq, "mla_kv_norm": mla_kv_norm,
        "mla_w_uk": mla_w_uk, "mla_w_uv": mla_w_uv,
        "diff_lam_q1": diff_lam_q1, "diff_lam_k1": diff_lam_k1,
        "diff_lam_q2": diff_lam_q2, "diff_lam_k2": diff_lam_k2, "diff_subln": diff_subln,
        "band_rel_bias": band_rel_bias, "w_out": w_out, "ln1_g": ln1_g, "ln1_b": ln1_b,
        "w_router_group": w_router_group, "b_router_group": b_router_group,
        "w_router_expert": w_router_expert, "b_router_expert": b_router_expert,
        "w_exp_gate": w_exp_gate, "w_exp_up": w_exp_up, "w_exp_down": w_exp_down,
        "ln2_g": ln2_g, "ln2_b": ln2_b,
    }
    weights = [_layer_weights(l, P) for l in range(w_in.shape[0])]
    y_prompt, st_p = _trunk(x_prompt, None, P, weights)
    caches = (cache_mla_ckv, cache_mla_kpe, cache_sb_k, cache_sb_v,
              cache_diff_k, cache_diff_v, cache_band_k, cache_band_v)
    y_sample, st_s = _trunk(x_sample, caches, P, weights)
    return (y_prompt, y_sample, *st_p, *st_s)
```

```python
import functools
import math

import jax
import jax.numpy as jnp
from jax import lax
from jax.experimental import pallas as pl
from jax.experimental.pallas import tpu as pltpu

F32 = jnp.float32
BF16 = jnp.bfloat16

CHUNK = 64
HEAD_DIM = 64
N_HEADS = 4
GROUP_WIDTH = N_HEADS * HEAD_DIM
PAIR = 2 * HEAD_DIM
MLA_Q_LORA = 384
MLA_KV_LORA = 256
MLA_NOPE = 64
MLA_ROPE = 32
MLA_QBLK = 256
DIFF_HALF = 32
BAND_ROWS = 512
REL_CLIP = 128
N_GROUPS = 4
EXPERTS_PER_GROUP = 4
N_EXPERTS = N_GROUPS * EXPERTS_PER_GROUP
D_EXPERT = 128
ROPE_THETA = 10000.0
LN_EPS = 1e-5
NEG = -1e30
LANES = 128
SB_DEAD = -104.0
V_WIDE = N_HEADS * LANES
KPE_COL = MLA_Q_LORA + MLA_KV_LORA + 9 * GROUP_WIDTH
D_IN_PAD = KPE_COL + LANES
MLA_ONE_LANE = PAIR + MLA_ROPE
EXP_GUARD = 60.0
LOG2E = 1.4426950408889634
VMEM_LIMIT = 56 * 1024 * 1024

_NT = (((1,), (1,)), ((), ()))


def _params(*sem):
    return pltpu.CompilerParams(dimension_semantics=sem, vmem_limit_bytes=VMEM_LIMIT)


def _dot(a, b):
    return jnp.dot(a, b, preferred_element_type=F32)


def _dot_nt(a, b):
    return lax.dot_general(a, b, _NT, preferred_element_type=F32)


def _layer_norm(x, g, b):
    mu = jnp.mean(x, axis=-1, keepdims=True)
    var = jnp.mean(jnp.square(x - mu), axis=-1, keepdims=True)
    return (x - mu) * lax.rsqrt(var + LN_EPS) * g + b


def _rms_norm(x, w, eps):
    return x * lax.rsqrt(jnp.mean(jnp.square(x), axis=-1, keepdims=True) + eps) * w


def _rope(x, cos, sin_signed):
    n = x.shape[-1]
    lane = lax.broadcasted_iota(jnp.int32, x.shape, 1)
    partner = jnp.where((lane & 16) == 0, pltpu.roll(x, n - 16, 1), pltpu.roll(x, 16, 1))
    return x * cos + partner * sin_signed


def _ones_row(shape):
    row = lax.broadcasted_iota(jnp.int32, shape, 0)
    return jnp.where((row & (LANES - 1)) == HEAD_DIM, 1.0, 0.0)


def _head_lanes(shape, lo, width):
    lane = lax.broadcasted_iota(jnp.int32, shape, 1)
    return (lane >= lo) & (lane < lo + width)


def _row_spec(tm, width):
    return pl.BlockSpec((tm, width), lambda i: (i, 0))


def _const_spec(shape):
    return pl.BlockSpec(shape, lambda *_: (0,) * len(shape))


def _resident_spec(rows, width, buffers=1):
    return pl.BlockSpec((1, rows, width), lambda b, i: (b, 0, 0), pipeline_mode=pl.Buffered(buffers))


def _ln_kernel(x_ref, g_ref, b_ref, o_ref):
    o_ref[...] = _layer_norm(x_ref[...], g_ref[...], b_ref[...])


def _ln_call(x, g, b, tm):
    n, d = x.shape
    return pl.pallas_call(
        _ln_kernel, grid=(n // tm,),
        in_specs=[_row_spec(tm, d), _const_spec((1, d)), _const_spec((1, d))],
        out_specs=_row_spec(tm, d),
        out_shape=jax.ShapeDtypeStruct((n, d), F32),
        compiler_params=_params("parallel"), name="ln_in",
    )(x, g.reshape(1, d), b.reshape(1, d))


_PRE_F32 = ("ckv", "kpe", "sbk", "sbv", "dk", "dv", "bk", "bv")
_PRE_BF16 = ("sbq", "sbk", "sbv", "dq", "dk", "dv", "bq", "bk", "bv")


def _pre_kernel(x_ref, cos_ref, sin_ref, win_ref, qn_ref, wuq_ref, kvn_ref, wdvt_ref, bk_zero, bv_zero,
                ckv_o, kpe_o, sbk_o, sbv_o, dk_o, dv_o, bk_o, bv_o,
                qm_o, sbq_h, sbk_h, sbv_h, dq_h, dk_h, dv_h, bq_h, bk_h, bv_h):
    xb = x_ref[...].astype(BF16)
    cos = cos_ref[...]
    sin = sin_ref[...]
    cos_l, sin_l = cos[:, :LANES], sin[:, :LANES]

    def proj(seg, width=GROUP_WIDTH):
        return _dot(xb, win_ref[:, seg:seg + width])

    mla_scale = (MLA_NOPE + MLA_ROPE) ** -0.5 * LOG2E
    cq =_rms_norm(proj(0, MLA_Q_LORA), qn_ref[...], 1e-6)
    q = _dot(cq.astype(BF16), wuq_ref[...])
    for h in range(N_HEADS):
        lo = h * MLA_QBLK
        qm_o[:, lo:lo + LANES] = (q[:, lo:lo + LANES] * mla_scale).astype(BF16)
        qr = _rope(q[:, lo + LANES:lo + MLA_QBLK], cos_l, sin_l)
        qm_o[:, lo + LANES:lo + MLA_QBLK] = (qr * mla_scale).astype(BF16)
    seg = MLA_Q_LORA
    ckv_o[...] = _rms_norm(proj(seg), kvn_ref[...], 1e-6)
    seg += MLA_KV_LORA
    kpe_o[...] = _rope(proj(KPE_COL, LANES), cos_l, sin_l)

    sbq_h[...] = (proj(seg) * HEAD_DIM ** -0.5).astype(BF16)
    k = proj(seg + GROUP_WIDTH)
    sbk_o[...] = k
    sbk_h[...] = k.astype(BF16)
    v = proj(seg + 2 * GROUP_WIDTH)
    sbv_o[...] = v
    sbv_h[...] = v.astype(BF16)
    seg += 3 * GROUP_WIDTH
    dq_h[...] = (_rope(proj(seg), cos, sin) * (DIFF_HALF ** -0.5 * LOG2E)).astype(BF16)
    k = _rope(proj(seg + GROUP_WIDTH), cos, sin)
    dk_o[...] = k
    dk_h[...] = k.astype(BF16)
    dv_o[...] = proj(seg + 2 * GROUP_WIDTH)
    dv_h[0] = (_dot_nt(wdvt_ref[...], xb) + _ones_row(dv_h.shape[1:])).astype(BF16)
    seg += 3 * GROUP_WIDTH
    bq_h[...] = (proj(seg) * (HEAD_DIM ** -0.5 * LOG2E)).astype(BF16)
    k = proj(seg + GROUP_WIDTH)
    bk_o[...] = k
    bk_h[0] = k.astype(BF16)
    v = proj(seg + 2 * GROUP_WIDTH)
    bv_o[...] = v
    bv_h[0] = v.astype(BF16)


def _pre_call(x, cos, sin, w, t, tm, front):
    n, d = x.shape
    nt = t // tm
    assert front % tm == 0
    tab = pl.BlockSpec((tm, GROUP_WIDTH), lambda i: (i % nt, 0))
    widths_f32 = [GROUP_WIDTH, LANES] + [GROUP_WIDTH] * 6
    widths_h = [N_HEADS * MLA_QBLK] + [GROUP_WIDTH] * 9
    out_shape = ([jax.ShapeDtypeStruct((n, wd), F32) for wd in widths_f32]
                 + [jax.ShapeDtypeStruct((n, wd), BF16) for wd in widths_h])
    out_specs = [_row_spec(tm, wd) for wd in widths_f32 + widths_h]
    dv = len(widths_f32) + 1 + _PRE_BF16.index("dv")
    out_shape[dv] = jax.ShapeDtypeStruct((n // t, V_WIDE, t), BF16)
    out_specs[dv] = pl.BlockSpec((1, V_WIDE, tm), lambda i: (i // nt, 0, i % nt))
    band = [len(widths_f32) + 1 + _PRE_BF16.index(name) for name in ("bk", "bv")]
    padded = jax.ShapeDtypeStruct((n // t, front + t, GROUP_WIDTH), BF16)
    for o in band:
        out_shape[o] = padded
        out_specs[o] = pl.BlockSpec((1, tm, GROUP_WIDTH), lambda i: (i // nt, front // tm + i % nt, 0))
    inputs = (x, cos, sin, w["w_in"], w["q_norm"], w["w_uq"], w["kv_norm"], w["w_dv_t"])
    zeros = jnp.zeros(padded.shape, BF16)
    outs = pl.pallas_call(
        _pre_kernel, grid=(n // tm,),
        in_specs=[_row_spec(tm, d), tab, tab,
                  _const_spec((d, D_IN_PAD)), _const_spec((1, MLA_Q_LORA)),
                  _const_spec((MLA_Q_LORA, N_HEADS * MLA_QBLK)), _const_spec((1, MLA_KV_LORA)),
                  _const_spec((V_WIDE, d)), pl.BlockSpec(memory_space=pl.ANY), pl.BlockSpec(memory_space=pl.ANY)],
        out_specs=out_specs,
        out_shape=out_shape,
        input_output_aliases={len(inputs): band[0], len(inputs) + 1: band[1]},
        compiler_params=_params("parallel"), name="pre",
    )(*inputs, zeros, zeros)
    f32 = dict(zip(_PRE_F32, outs[:8]))
    h16 = dict(zip(("qm",) + _PRE_BF16, outs[8:]))
    return f32, h16


def _kvup_kernel(ckv_ref, kpe_ref, wuk_ref, wuv_ref, km_o, vm_o):
    cb = ckv_ref[...].astype(BF16)
    kn = _dot(cb, wuk_ref[...]).astype(BF16)
    lane = lax.broadcasted_iota(jnp.int32, kpe_ref.shape, 1)
    kpe = jnp.where(lane == MLA_ROPE, 1.0, kpe_ref[...]).astype(BF16)
    for p in range(2):
        km_o[:, p * MLA_QBLK:p * MLA_QBLK + PAIR] = kn[:, p * PAIR:(p + 1) * PAIR]
        km_o[:, p * MLA_QBLK + PAIR:(p + 1) * MLA_QBLK] = kpe
    vm_o[0] = (_dot_nt(wuv_ref[...], cb) + _ones_row(vm_o.shape[1:])).astype(BF16)


def _kvup_call(ckv, kpe, w, tm, rows):
    n = ckv.shape[0]
    nt = rows // tm
    return pl.pallas_call(
        _kvup_kernel, grid=(n // tm,),
        in_specs=[_row_spec(tm, MLA_KV_LORA), _row_spec(tm, LANES),
                  _const_spec((MLA_KV_LORA, GROUP_WIDTH)), _const_spec((V_WIDE, MLA_KV_LORA))],
        out_specs=[_row_spec(tm, 2 * MLA_QBLK), pl.BlockSpec((1, V_WIDE, tm), lambda i: (i // nt, 0, i % nt))],
        out_shape=[jax.ShapeDtypeStruct((n, 2 * MLA_QBLK), BF16),
                   jax.ShapeDtypeStruct((n // rows, V_WIDE, rows), BF16)],
        compiler_params=_params("parallel"), name="kvup",
    )(ckv, kpe, w["w_uk"], w["w_uv"])


def _fold_max(d):
    m = d[:, :LANES]
    for c in range(1, d.shape[1] // LANES):
        m = jnp.maximum(m, d[:, c * LANES:(c + 1) * LANES])
    return jnp.max(m.reshape(m.shape[0] // 8, 8, m.shape[1]), axis=0)


def _softmax_chains(chains, operands, i, jd, tq, tk, off, qs_ref, acc_ref):
    sub = max(tq, LANES)
    qstart = off + i * tq
    own = pl.multiple_of(qstart // sub * sub, sub)
    own_mask = ((own + lax.broadcasted_iota(jnp.int32, (tq, sub), 1)) // CHUNK
                <= (qstart + lax.broadcasted_iota(jnp.int32, (tq, sub), 0)) // CHUNK)
    own_mask_t = ((own + lax.broadcasted_iota(jnp.int32, (sub, tq), 0)) // CHUNK
                  <= (qstart + lax.broadcasted_iota(jnp.int32, (sub, tq), 1)) // CHUNK)

    def sweep(step, carry):
        carry = lax.fori_loop(0, jd, lambda j, cy: step(pl.multiple_of(j * tk, tk), tk, False, cy), carry)
        carry = lax.fori_loop(0, (own - jd * tk) // sub,
                              lambda s, cy: step(pl.multiple_of(jd * tk + s * sub, sub), sub, False, cy), carry)
        return step(own, sub, True, carry)

    for n in range(chains):
        q, k, _ = operands(n, own, sub)
        m0 = jnp.max(jnp.where(own_mask, _dot_nt(q, k), NEG), axis=-1, keepdims=True)
        lane = lax.broadcasted_iota(jnp.int32, q.shape, 1)
        qs_ref[n] = jnp.where(lane == operands(n, None), (-m0).astype(BF16), q)
    acc_ref[...] = jnp.zeros(acc_ref.shape, F32)

    def fast(start, rows, masked, top):
        d_next = _dot_nt(operands(0, start, rows)[1], qs_ref[0])
        for n in range(chains):
            d = d_next
            if n + 1 < chains:
                d_next = _dot_nt(operands(n + 1, start, rows)[1], qs_ref[n + 1])
            if masked:
                d = jnp.where(own_mask_t, d, NEG)
            top = jnp.maximum(top, _fold_max(d))
            acc_ref[n] += _dot(operands(n, start, rows)[2], jnp.exp2(d).astype(BF16))
        return top

    top = sweep(fast, jnp.full((8, min(tq, LANES)), NEG, F32))

    @pl.when(jnp.max(top) > EXP_GUARD)
    def _():
        def safe(start, rows, masked, states):
            new = []
            for n in range(chains):
                q, k, vt = operands(n, start, rows)
                s = _dot_nt(k, q)
                if masked:
                    s = jnp.where(own_mask_t, s, NEG)
                m, acc = states[n]
                m_new = jnp.maximum(m, jnp.max(s, axis=0, keepdims=True))
                new.append((m_new, jnp.exp2(m - m_new) * acc + _dot(vt, jnp.exp2(s - m_new).astype(BF16))))
            return tuple(new)

        states = sweep(safe, tuple((jnp.full((1, tq), NEG, F32), jnp.zeros((LANES, tq), F32))
                                   for _ in range(chains)))
        for n in range(chains):
            acc_ref[n] = states[n][1]


def _flash_scratch(chains, tq):
    return [pltpu.VMEM((chains, tq, MLA_QBLK), BF16), pltpu.VMEM((chains, LANES, tq), F32)]


def _mla_kernel(q_ref, k_ref, v_ref, o_ref, qs_ref, acc_ref, *, tq, tk, off):
    i = pl.program_id(1)
    jd = (off + i * tq) // tk

    def operands(h, start, rows=tk):
        if start is None:
            return MLA_ONE_LANE
        p = h // 2
        return (q_ref[0, :, h * MLA_QBLK:(h + 1) * MLA_QBLK],
                k_ref[0, pl.ds(start, rows), p * MLA_QBLK:(p + 1) * MLA_QBLK],
                v_ref[0, h * LANES:(h + 1) * LANES, pl.ds(start, rows)])

    _softmax_chains(N_HEADS, operands, i, jd, tq, tk, off, qs_ref, acc_ref)
    for p in range(2):
        pair = []
        for h in (2 * p, 2 * p + 1):
            acc = acc_ref[h]
            pair.append(acc[:HEAD_DIM] / acc[HEAD_DIM:HEAD_DIM + 1])
        o_ref[0, :, p * PAIR:(p + 1) * PAIR] = jnp.concatenate(pair, axis=0).T.astype(BF16)


def _diff_kernel(q_ref, k_ref, v_ref, lam_ref, subln_ref, o_ref, qs_ref, acc_ref, *, tq, tk, off, lam_init):
    i = pl.program_id(1)
    jd = (off + i * tq) // tk

    def operands(n, start, rows=tk):
        if start is None:
            return PAIR
        h, half = n // 2, n % 2
        p, hh = h // 2, h % 2
        qp = q_ref[0, :, p * PAIR:(p + 1) * PAIR]
        sel = _head_lanes((tq, PAIR), hh * HEAD_DIM + half * DIFF_HALF, DIFF_HALF)
        zero = jnp.zeros_like(qp)
        one = jnp.where(lax.broadcasted_iota(jnp.int32, (rows, PAIR), 1) == 0, 1.0, 0.0).astype(BF16)
        return (jnp.concatenate([jnp.where(sel, qp, zero), zero], axis=-1),
                jnp.concatenate([k_ref[0, pl.ds(start, rows), p * PAIR:(p + 1) * PAIR], one], axis=-1),
                v_ref[0, h * LANES:(h + 1) * LANES, pl.ds(start, rows)])

    _softmax_chains(2 * N_HEADS, operands, i, jd, tq, tk, off, qs_ref, acc_ref)

    lp = lam_ref[...]
    lam = (jnp.exp(jnp.sum(lp[0:1] * lp[1:2], axis=-1, keepdims=True))
           - jnp.exp(jnp.sum(lp[2:3] * lp[3:4], axis=-1, keepdims=True)) + lam_init)
    gain = subln_ref[...]
    for p in range(2):
        pair = []
        for h in (2 * p, 2 * p + 1):
            a1, a2 = acc_ref[2 * h], acc_ref[2 * h + 1]
            o = (a1[:HEAD_DIM] / a1[HEAD_DIM:HEAD_DIM + 1]
                 - lam * (a2[:HEAD_DIM] / a2[HEAD_DIM:HEAD_DIM + 1]))
            ms = jnp.mean(o * o, axis=0, keepdims=True)
            pair.append(o * lax.rsqrt(ms + 1e-5) * gain * (1.0 - lam_init))
        o_ref[0, :, p * PAIR:(p + 1) * PAIR] = jnp.concatenate(pair, axis=0).T.astype(BF16)


def _sb_scratch(tq):
    return [pltpu.VMEM((N_HEADS, tq, PAIR), BF16), pltpu.VMEM((N_HEADS, tq, 1), F32),
            pltpu.VMEM((N_HEADS, tq, PAIR), F32)]


def _sb_kernel(q_ref, k_ref, v_ref, o_ref, qs_ref, later_ref, acc_ref, *, tq, tk, off):
    i = pl.program_id(1)
    jd = (off + i * tq) // tk
    row = lax.broadcasted_iota(jnp.int32, (tk, tk), 0)
    col = lax.broadcasted_iota(jnp.int32, (tk, tk), 1)
    after = jnp.where(row > col, 1.0, 0.0).astype(BF16)
    qpos = off + i * tq + lax.broadcasted_iota(jnp.int32, (tq, tk), 0)
    kcol = lax.broadcasted_iota(jnp.int32, (tq, tk), 1)
    first = _head_lanes((tq, PAIR), 0, HEAD_DIM)

    for h in range(N_HEADS):
        qp = q_ref[0, :, (h // 2) * PAIR:(h // 2 + 1) * PAIR]
        qs_ref[h] = jnp.where(_head_lanes((tq, PAIR), (h % 2) * HEAD_DIM, HEAD_DIM), qp, jnp.zeros_like(qp))
    later_ref[...] = jnp.zeros(later_ref.shape, F32)
    acc_ref[...] = jnp.zeros(acc_ref.shape, F32)

    def block(j, diagonal):
        start = pl.multiple_of(j * tk, tk)

        def logits(h):
            return _dot_nt(qs_ref[h], k_ref[0, pl.ds(start, tk), (h // 2) * PAIR:(h // 2 + 1) * PAIR])

        z_next = logits(0)
        for h in range(N_HEADS):
            vp = v_ref[0, pl.ds(start, tk), (h // 2) * PAIR:(h // 2 + 1) * PAIR]
            z = z_next
            if h + 1 < N_HEADS:
                z_next = logits(h + 1)
            t = jnp.log(1.0 + jnp.exp(-jnp.abs(z)))
            log_keep = -(jnp.maximum(z, 0.0) + t)
            log_beta = jnp.minimum(z, 0.0) - t
            if diagonal:
                earlier = (start + kcol) < qpos
                log_keep = jnp.where(earlier, log_keep, 0.0)
            hi = log_keep.astype(BF16)
            lo = (log_keep - hi.astype(F32)).astype(BF16)
            within = _dot(hi, after) + _dot(lo, after)
            w = jnp.exp(log_beta + within + later_ref[h])
            if diagonal:
                w = jnp.where(earlier, w, 0.0)
            acc_ref[h] += _dot(w.astype(BF16), vp)
            later_ref[h] += jnp.sum(log_keep, axis=-1, keepdims=True)
        return (jnp.max(later_ref[...]) > SB_DEAD).astype(jnp.int32)

    def body(c):
        j, _ = c
        return j - 1, block(j, False)

    lax.while_loop(lambda c: (c[0] >= 0) & (c[1] > 0), body, (jd - 1, block(jd, True)))
    for p in range(2):
        o_ref[0, :, p * PAIR:(p + 1) * PAIR] = jnp.where(first, acc_ref[2 * p], acc_ref[2 * p + 1]).astype(BF16)


def _causal_call(kernel, q, k, v, extra, extra_specs, tq, tk, off, name, scratch):
    b, t, qw = q.shape
    tk_all = k.shape[1]
    resident = k.size * k.dtype.itemsize // b + v.size * v.dtype.itemsize // b
    buffers = 2 if 2 * resident <= VMEM_LIMIT * 5 // 8 else 1
    assert off % tk == 0 and tk % tq == 0 and t % tq == 0 and tk_all % tk == 0
    assert off + t <= tk_all
    return pl.pallas_call(
        functools.partial(kernel, tq=tq, tk=tk, off=off), grid=(b, t // tq),
        in_specs=[pl.BlockSpec((1, tq, qw), lambda b_, i: (b_, i, 0)),
                  _resident_spec(tk_all, k.shape[2], buffers),
                  _resident_spec(v.shape[1], v.shape[2], buffers)] + extra_specs,
        out_specs=pl.BlockSpec((1, tq, GROUP_WIDTH), lambda b_, i: (b_, i, 0)),
        out_shape=jax.ShapeDtypeStruct((b, t, GROUP_WIDTH), BF16),
        scratch_shapes=scratch,
        compiler_params=_params("parallel", "arbitrary"), name=name,
    )(q, k, v, *extra)


def _band_kernel(tab_ref, q_ref, k_ref, v_ref, o_ref, bias_ref, base_ref, *, tq, win, pos0):
    b = pl.program_id(0)
    i = pl.program_id(1)
    rows = 8
    wext = base_ref.shape[2]

    @pl.when((b == 0) & (i == 0))
    def _():
        rr = lax.broadcasted_iota(jnp.int32, (rows, wext), 0)
        x = lax.broadcasted_iota(jnp.int32, (rows, wext), 1)
        rel = jnp.clip(rr - x + tq + BAND_ROWS, -REL_CLIP, REL_CLIP) + REL_CLIP

        def pick(d, vals):
            hit = rel == d
            return tuple(jnp.where(hit, tab_ref[h, d], vals[h]) for h in range(N_HEADS))

        vals = lax.fori_loop(0, 2 * REL_CLIP + 1, pick,
                             tuple(jnp.zeros((rows, wext), F32) for _ in range(N_HEADS)))
        for h in range(N_HEADS):
            base_ref[h] = vals[h] * LOG2E

        def fill(c, carry):
            r0 = pl.multiple_of(c * rows, rows)
            r = r0 + lax.broadcasted_iota(jnp.int32, (rows, win), 0)
            u = lax.broadcasted_iota(jnp.int32, (rows, win), 1)
            dchunk = r // CHUNK - u // CHUNK + BAND_ROWS // CHUNK
            ok = (dchunk >= 0) & (dchunk <= BAND_ROWS // CHUNK)
            for h in range(N_HEADS):
                shifted = pltpu.roll(base_ref[h], wext - tq + r0, 1)[:, :win]
                bias_ref[h, pl.ds(r0, rows), :] = jnp.where(ok, shifted, NEG)
            return carry

        lax.fori_loop(0, tq // rows, fill, 0)

    start = pl.multiple_of(i * tq, tq)
    u = lax.broadcasted_iota(jnp.int32, (win, PAIR), 0)
    lane = lax.broadcasted_iota(jnp.int32, (win, PAIR), 1)
    missing = jnp.where(((pos0 + i * tq - BAND_ROWS + u) < 0) & (lane == 0), NEG, 0.0).astype(BF16)
    q_one = jnp.where(lax.broadcasted_iota(jnp.int32, (tq, PAIR), 1) == 0, 1.0, 0.0).astype(BF16)
    first = _head_lanes((tq, PAIR), 0, HEAD_DIM)

    def scores(h):
        p, hh = h // 2, h % 2
        qp = q_ref[0, :, p * PAIR:(p + 1) * PAIR]
        kw = jnp.concatenate([k_ref[0, pl.ds(start, win), p * PAIR:(p + 1) * PAIR], missing], axis=-1)
        qh = jnp.where(_head_lanes((tq, PAIR), hh * HEAD_DIM, HEAD_DIM), qp, jnp.zeros_like(qp))
        return _dot_nt(jnp.concatenate([qh, q_one], axis=-1), kw)

    outs = []
    s_next = scores(0)
    for h in range(N_HEADS):
        s = s_next + bias_ref[h]
        if h + 1 < N_HEADS:
            s_next = scores(h + 1)
        e = jnp.exp2(s - jnp.max(s, axis=-1, keepdims=True))
        vw = v_ref[0, pl.ds(start, win), (h // 2) * PAIR:(h // 2 + 1) * PAIR]
        outs.append(_dot(e.astype(BF16), vw) / jnp.sum(e, axis=-1, keepdims=True))
    for p in range(2):
        o_ref[0, :, p * PAIR:(p + 1) * PAIR] = jnp.where(first, outs[2 * p], outs[2 * p + 1]).astype(BF16)


def _band_call(q, k_pad, v_pad, rel_table, tq, win, pos0):
    b, t, _ = q.shape
    rows = k_pad.shape[1]
    assert t % tq == 0 and tq % CHUNK == 0 and win % LANES == 0 and win >= tq + BAND_ROWS
    assert rows >= t - tq + win and pos0 % CHUNK == 0
    return pl.pallas_call(
        functools.partial(_band_kernel, tq=tq, win=win, pos0=pos0), grid=(b, t // tq),
        in_specs=[pl.BlockSpec(memory_space=pltpu.SMEM),
                  pl.BlockSpec((1, tq, GROUP_WIDTH), lambda b_, i: (b_, i, 0)),
                  _resident_spec(rows, GROUP_WIDTH, 2), _resident_spec(rows, GROUP_WIDTH, 2)],
        out_specs=pl.BlockSpec((1, tq, GROUP_WIDTH), lambda b_, i: (b_, i, 0)),
        out_shape=jax.ShapeDtypeStruct((b, t, GROUP_WIDTH), BF16),
        scratch_shapes=[pltpu.VMEM((N_HEADS, tq, win), F32),
                        pltpu.VMEM((N_HEADS, 8, -(-(win + tq) // LANES) * LANES), F32)],
        compiler_params=_params("arbitrary", "arbitrary"), name="band",
    )(rel_table, q, k_pad, v_pad)


def _post_kernel(h_ref, oa_ref, ob_ref, oc_ref, od_ref, wout_ref, g1_ref, b1_ref,
                 wr_hi_ref, wr_lo_ref, br_ref, wg_ref, wu_ref, wd_ref, g2_ref, b2_ref, o_ref, *, alpha):
    tm = h_ref.shape[0]
    y = jnp.zeros(h_ref.shape, F32)
    for g, ref in enumerate((oa_ref, ob_ref, oc_ref, od_ref)):
        y = y + _dot(ref[...], wout_ref[g * GROUP_WIDTH:(g + 1) * GROUP_WIDTH, :])
    x = _layer_norm(alpha * h_ref[...] + y, g1_ref[...], b1_ref[...])

    x_hi = x.astype(BF16)
    x_lo = (x - x_hi.astype(F32)).astype(BF16)
    logit = (_dot(x_hi, wr_hi_ref[...]) + _dot(x_lo, wr_hi_ref[...]) + _dot(x_hi, wr_lo_ref[...])
             + br_ref[...])
    lane = lax.broadcasted_iota(jnp.int32, (tm, LANES), 1).astype(F32)
    ninf = -jnp.inf

    def lane_max(a):
        return jnp.max(a, axis=-1, keepdims=True)

    def first_lane(hit):
        return jnp.min(jnp.where(hit, lane, float(LANES)), axis=-1, keepdims=True)

    gl = jnp.where(lane < N_GROUPS, logit, ninf)
    g_max = lane_max(gl)
    g_idx = first_lane(gl == g_max)
    g_weight = 1.0 / jnp.sum(jnp.exp(gl - g_max), axis=-1, keepdims=True)
    e_lo = N_GROUPS + EXPERTS_PER_GROUP * g_idx
    el = jnp.where((lane >= e_lo) & (lane < e_lo + EXPERTS_PER_GROUP), logit, ninf)
    v1 = lane_max(el)
    i1 = first_lane(el == v1)
    el2 = jnp.where(lane == i1, ninf, el)
    v2 = lane_max(el2)
    i2 = first_lane(el2 == v2)
    e2 = jnp.exp(v2 - v1)
    w1 = g_weight / (1.0 + e2)
    w2 = g_weight * e2 / (1.0 + e2)
    comb = jnp.where(lane == i1, w1, 0.0) + jnp.where(lane == i2, w2, 0.0)

    xb = x_hi
    gate = _dot(xb, wg_ref[...])
    up = _dot(xb, wu_ref[...])
    hidden = []
    for e in range(N_EXPERTS):
        c_e = jnp.sum(jnp.where(lane == float(N_GROUPS + e), comb, 0.0), axis=-1, keepdims=True)
        ge = gate[:, e * D_EXPERT:(e + 1) * D_EXPERT]
        ue = up[:, e * D_EXPERT:(e + 1) * D_EXPERT]
        hidden.append(((ge * (1.0 / (1.0 + jnp.exp(-ge)))) * ue * c_e).astype(BF16))
    f = _dot(jnp.concatenate(hidden, axis=-1), wd_ref[...])
    o_ref[...] = _layer_norm(alpha * x + f, g2_ref[...], b2_ref[...])


def _post_call(h, mix, w, tm, alpha):
    n, d = h.shape
    de = N_EXPERTS * D_EXPERT
    return pl.pallas_call(
        functools.partial(_post_kernel, alpha=alpha), grid=(n // tm,),
        in_specs=[_row_spec(tm, d)] + [_row_spec(tm, GROUP_WIDTH)] * 4
        + [_const_spec((d, d)), _const_spec((1, d)), _const_spec((1, d)),
           _const_spec((d, LANES)), _const_spec((d, LANES)), _const_spec((1, LANES)),
           _const_spec((d, de)), _const_spec((d, de)), _const_spec((de, d)),
           _const_spec((1, d)), _const_spec((1, d))],
        out_specs=_row_spec(tm, d),
        out_shape=jax.ShapeDtypeStruct((n, d), F32),
        compiler_params=_params("parallel"), name="post",
    )(h, *mix, w["w_out"], w["ln1_g"], w["ln1_b"], w["wr_hi"], w["wr_lo"], w["b_r"],
      w["w_gate"], w["w_up"], w["w_down"], w["ln2_g"], w["ln2_b"])


def _wide_columns(w):
    rows = w.shape[0]
    w = w.reshape(rows, N_HEADS, HEAD_DIM)
    return jnp.concatenate([w, jnp.zeros_like(w)], axis=-1).reshape(rows, V_WIDE)


def _wide_values_t(v):
    one = jnp.ones(v.shape[:-1] + (1,), v.dtype)
    zero = jnp.zeros(v.shape[:-1] + (LANES - HEAD_DIM - 1,), v.dtype)
    wide = jnp.concatenate([v, one, zero], axis=-1).reshape(v.shape[0], v.shape[1], V_WIDE)
    return wide.transpose(0, 2, 1)


def _layer_weights(l, P):
    d = P["w_in"].shape[1]
    w_in = P["w_in"][l]
    kpe_lo = MLA_Q_LORA + MLA_KV_LORA
    dv_lo = kpe_lo + MLA_ROPE + 5 * GROUP_WIDTH
    w_in_pad = jnp.concatenate(
        [w_in[:, :kpe_lo], w_in[:, kpe_lo + MLA_ROPE:], w_in[:, kpe_lo:kpe_lo + MLA_ROPE],
         jnp.zeros((d, LANES - MLA_ROPE), F32)], axis=1)
    w_uq = P["mla_w_uq"][l]
    blocks = []
    for h in range(N_HEADS):
        src = h * (MLA_NOPE + MLA_ROPE)
        nope = w_uq[:, src:src + MLA_NOPE]
        zero = jnp.zeros_like(nope)
        blocks += ([nope, zero] if h % 2 == 0 else [zero, nope])
        blocks += [w_uq[:, src + MLA_NOPE:src + MLA_NOPE + MLA_ROPE],
                   jnp.zeros((MLA_Q_LORA, LANES - MLA_ROPE), F32)]
    w_r = jnp.concatenate([P["w_router_group"][l], P["w_router_expert"][l],
                           jnp.zeros((d, LANES - N_GROUPS - N_EXPERTS), F32)], axis=1)
    wr_hi = w_r.astype(BF16)
    b_r = jnp.concatenate([P["b_router_group"][l], P["b_router_expert"][l],
                           jnp.zeros((LANES - N_GROUPS - N_EXPERTS,), F32)]).reshape(1, LANES)
    de = N_EXPERTS * D_EXPERT
    return {
        "w_in": w_in_pad.astype(BF16),
        "q_norm": P["mla_q_norm"][l].reshape(1, -1),
        "w_uq": jnp.concatenate(blocks, axis=1).astype(BF16),
        "kv_norm": P["mla_kv_norm"][l].reshape(1, -1),
        "w_uk": P["mla_w_uk"][l].astype(BF16),
        "w_uv": _wide_columns(P["mla_w_uv"][l]).T.astype(BF16),
        "w_dv_t": _wide_columns(w_in[:, dv_lo:dv_lo + GROUP_WIDTH]).T.astype(BF16),
        "lam": jnp.stack([P["diff_lam_q1"][l], P["diff_lam_k1"][l],
                          P["diff_lam_q2"][l], P["diff_lam_k2"][l]]),
        "subln": P["diff_subln"][l].reshape(HEAD_DIM, 1),
        "rel": P["band_rel_bias"][l],
        "w_out": P["w_out"][l].astype(BF16),
        "ln1_g": P["ln1_g"][l].reshape(1, -1), "ln1_b": P["ln1_b"][l].reshape(1, -1),
        "wr_hi": wr_hi, "wr_lo": (w_r - wr_hi.astype(F32)).astype(BF16), "b_r": b_r,
        "w_gate": P["w_exp_gate"][l].transpose(2, 0, 1, 3).reshape(d, de).astype(BF16),
        "w_up": P["w_exp_up"][l].transpose(2, 0, 1, 3).reshape(d, de).astype(BF16),
        "w_down": P["w_exp_down"][l].reshape(de, d).astype(BF16),
        "ln2_g": P["ln2_g"][l].reshape(1, -1), "ln2_b": P["ln2_b"][l].reshape(1, -1),
    }


def _rope_tables(pos):
    inv_freq = ROPE_THETA ** (-jnp.arange(0, MLA_ROPE, 2, dtype=F32) / MLA_ROPE)
    ang = pos.astype(F32)[:, None] * inv_freq[None, :]
    cos, sin = jnp.cos(ang), jnp.sin(ang)
    reps = GROUP_WIDTH // MLA_ROPE
    return (jnp.tile(jnp.concatenate([cos, cos], axis=1), (1, reps)),
            jnp.tile(jnp.concatenate([-sin, sin], axis=1), (1, reps)))


def _flat_heads(a):
    return a.reshape(a.shape[0], a.shape[1], GROUP_WIDTH)


def _pad_rows(a, rows):
    return jnp.pad(a, ((0, 0), (0, rows - a.shape[1]), (0, 0)))


def _trunk(x, caches, P, weights):
    b, t, d = x.shape
    depth = len(weights)
    alpha = (2 * depth) ** 0.25
    n = b * t
    past = 0 if caches is None else caches[0].shape[2]
    if caches is None:
        tm, tq, tk, band_tq = 256, 512, 1024, 256
        sb_tq = sb_tk = 256
        tk_all = t
    else:
        tm, tq, band_tq = t, t, t
        tk = next(c for c in (512, 256, 2 * t) if past % c == 0)
        sb_tq, sb_tk = t, 2 * t
        tk_all = -(-(past + t) // tk) * tk
        assert caches[6].shape[2] == BAND_ROWS
    assert t >= BAND_ROWS or caches is not None
    band_win = -(-(band_tq + BAND_ROWS) // LANES) * LANES
    band_rows = t - band_tq + band_win
    cos, sin = _rope_tables(past + jnp.arange(t, dtype=jnp.int32))
    h = _ln_call(x.reshape(n, d), P["ln_in_g"], P["ln_in_b"], tm)
    new_rows = [[] for _ in range(8)]
    for l in range(depth):
        w = weights[l]
        f32, h16 = _pre_call(h, cos, sin, w, t, tm, BAND_ROWS if caches is None else 0)

        def new3(a):
            return a.reshape(b, t, a.shape[-1])

        if caches is None:
            def keys(name, cache_idx):
                return new3(h16[name])
            diff_v = h16["dv"]
            ckv_all, kpe_all = f32["ckv"], f32["kpe"]
            assert band_rows == BAND_ROWS + t
            band_k, band_v = h16["bk"], h16["bv"]
        else:
            c = [a[l] for a in caches]

            def keys(name, cache_idx):
                old = _flat_heads(c[cache_idx]).astype(BF16)
                return _pad_rows(jnp.concatenate([old, new3(h16[name])], axis=1), tk_all)
            diff_v = jnp.concatenate([_wide_values_t(c[5]).astype(BF16), h16["dv"]], axis=2)
            diff_v = jnp.pad(diff_v, ((0, 0), (0, 0), (0, tk_all - past - t)))
            ckv_all =_pad_rows(jnp.concatenate([c[0], new3(f32["ckv"])], axis=1), tk_all)
            ckv_all = ckv_all.reshape(b * tk_all, MLA_KV_LORA)
            kpe_old = jnp.pad(c[1], ((0, 0), (0, 0), (0, LANES - MLA_ROPE)))
            kpe_all = _pad_rows(jnp.concatenate([kpe_old, new3(f32["kpe"])], axis=1), tk_all)
            kpe_all = kpe_all.reshape(b * tk_all, LANES)
            band_k = _pad_rows(jnp.concatenate([_flat_heads(c[6]).astype(BF16), h16["bk"]], axis=1), band_rows)
            band_v = _pad_rows(jnp.concatenate([_flat_heads(c[7]).astype(BF16), h16["bv"]], axis=1), band_rows)

        km, vm = _kvup_call(ckv_all, kpe_all, w, tm if caches is None else tk, tk_all)
        o_a = _causal_call(_mla_kernel, new3(h16["qm"]), km.reshape(b, tk_all, -1), vm,
                           [], [], tq, tk, past, "mla", _flash_scratch(N_HEADS, tq))
        o_b = _causal_call(_sb_kernel, new3(h16["sbq"]), keys("sbk", 2), keys("sbv", 3), [], [],
                           sb_tq, sb_tk, past, "sb", _sb_scratch(sb_tq))
        lam_init = 0.8 - 0.6 * math.exp(-0.3 * l)
        o_c = _causal_call(functools.partial(_diff_kernel, lam_init=lam_init), new3(h16["dq"]),
                           keys("dk", 4), diff_v, [w["lam"], w["subln"]],
                           [_const_spec((4, DIFF_HALF)), _const_spec((HEAD_DIM, 1))], tq, tk, past, "diff",
                           _flash_scratch(2 * N_HEADS, tq))
        o_d = _band_call(new3(h16["bq"]), band_k, band_v, w["rel"], band_tq, band_win, past)
        mix = [o.reshape(n, GROUP_WIDTH) for o in (o_a, o_b, o_c, o_d)]
        h = _post_call(h, mix, w, tm, alpha)

        keep = min(BAND_ROWS, t)
        rows = (new3(f32["ckv"]), new3(f32["kpe"])[:, :, :MLA_ROPE],
                new3(f32["sbk"]), new3(f32["sbv"]), new3(f32["dk"]), new3(f32["dv"]),
                new3(f32["bk"])[:, t - keep:], new3(f32["bv"])[:, t - keep:])
        for i, (lst, arr) in enumerate(zip(new_rows, rows)):
            lst.append(arr if i < 2 else arr.reshape(b, arr.shape[1], N_HEADS, HEAD_DIM))
    return h.reshape(b, t, d), [jnp.stack(lst) for lst in new_rows]


def kernel(x_prompt, x_sample, cache_mla_ckv, cache_mla_kpe, cache_sb_k, cache_sb_v, cache_diff_k, cache_diff_v, cache_band_k, cache_band_v, ln_in_g, ln_in_b, w_in, mla_q_norm, mla_w_uq, mla_kv_norm, mla_w_uk, mla_w_uv, diff_lam_q1, diff_lam_k1, diff_lam_q2, diff_lam_k2, diff_subln, band_rel_bias, w_out, ln1_g, ln1_b, w_router_group, b_router_group, w_router_expert, b_router_expert, w_exp_gate, w_exp_up, w_exp_down, ln2_g, ln2_b):
    P = {
        "ln_in_g": ln_in_g, "ln_in_b": ln_in_b, "w_in": w_in,
        "mla_q_norm": mla_q_norm, "mla_w_uq": mla_w_uq, "mla_kv_norm": mla_kv_norm,
        "mla_w_uk": mla_w_uk, "mla_w_uv": mla_w_uv,
        "diff_lam_q1": diff_lam_q1, "diff_lam_k1": diff_lam_k1,
        "diff_lam_q2": diff_lam_q2, "diff_lam_k2": diff_lam_k2, "diff_subln": diff_subln,
        "band_rel_bias": band_rel_bias, "w_out": w_out, "ln1_g": ln1_g, "ln1_b": ln1_b,
        "w_router_group": w_router_group, "b_router_group": b_router_group,
        "w_router_expert": w_router_expert, "b_router_expert": b_router_expert,
        "w_exp_gate": w_exp_gate, "w_exp_up": w_exp_up, "w_exp_down": w_exp_down,
        "ln2_g": ln2_g, "ln2_b": ln2_b,
    }
    weights = [_layer_weights(l, P) for l in range(w_in.shape[0])]
    y_prompt, st_p = _trunk(x_prompt, None, P, weights)
    caches = (cache_mla_ckv, cache_mla_kpe, cache_sb_k, cache_sb_v,
              cache_diff_k, cache_diff_v, cache_band_k, cache_band_v)
    y_sample, st_s = _trunk(x_sample, caches, P, weights)
    return (y_prompt, y_sample, *st_p, *st_s)
```
